```python
import math
import jax, jax.numpy as jnp
from jax import lax
import numpy as np

D_MODEL = 1024
BATCH = 16
SEQ = 4096
DEPTH = 1
DEC_BATCH = 4
DEC_SEQ = 8192
PAST_LEN = 128

D_LRU = D_MODEL
LRU_HEADS = 16
LRU_HEAD_DIM = D_LRU // LRU_HEADS
CONV_WIDTH = 4
CONV_PAD = (2, 1)
LRU_C = 8.0
D_SSM = D_MODEL
SSM_GROUP = 16
SSM_GROUPS = D_SSM // SSM_GROUP
SSM_STATE = 64
N_KEYS = 128
N_EXPERTS = N_KEYS * N_KEYS
PEER_HEADS = 8
PEER_TOPK = 16
PEER_QDIM = 256
PEER_HALF = PEER_QDIM // 2
PEER_BLOCK = 128
EPS = 1e-6
N_IN = 2 * D_LRU + D_SSM + 2 * D_MODEL

kernel_name = 'hybrid_lru_s5_peer_encoder'

F32 = jnp.float32


def rmsnorm(x, g):
    xf = x.astype(F32)
    y = xf * lax.rsqrt(jnp.mean(xf * xf, axis=-1, keepdims=True) + EPS) * g.astype(F32)
    return y.astype(x.dtype)


def _real_combine(c1, c2):
    a1, b1 = c1
    a2, b2 = c2
    return a1 * a2, a2 * b1 + b2


def _complex_combine(c1, c2):
    ar1, ai1, br1, bi1 = c1
    ar2, ai2, br2, bi2 = c2
    ar = ar1 * ar2 - ai1 * ai2
    ai = ar1 * ai2 + ai1 * ar2
    br = ar2 * br1 - ai2 * bi1 + br2
    bi = ar2 * bi1 + ai2 * br1 + bi2
    return ar, ai, br, bi


def rglru_branch(xa, conv_w, conv_b, wr, br, wi, bi, lam):
    B_, L_, _ = xa.shape
    xc = lax.conv_general_dilated(xa, conv_w[:, None, :].astype(xa.dtype), window_strides=(1,),
                                  padding=[CONV_PAD], dimension_numbers=('NWC', 'WIO', 'NWC'),
                                  feature_group_count=D_LRU)
    xc = xc.astype(F32) + conv_b.astype(F32)
    xh = xc.reshape(B_, L_, LRU_HEADS, LRU_HEAD_DIM)
    r = jax.nn.sigmoid(jnp.einsum('blhi,dhij->dblhj', xh, wr.astype(F32)).reshape(2, B_, L_, D_LRU)
                       + br.astype(F32)[:, None, None, :])
    i = jax.nn.sigmoid(jnp.einsum('blhi,dhij->dblhj', xh, wi.astype(F32)).reshape(2, B_, L_, D_LRU)
                       + bi.astype(F32)[:, None, None, :])
    log_a = -LRU_C * r * jax.nn.softplus(-lam.astype(F32))[:, None, None, :]
    a = jnp.exp(log_a)
    b = jnp.sqrt(-jnp.expm1(2.0 * log_a)) * i * xc[None]
    _, h_f = lax.associative_scan(_real_combine, (a[0], b[0]), axis=1)
    _, h_b = lax.associative_scan(_real_combine, (a[1], b[1]), axis=1, reverse=True)
    return h_f + h_b


def s5_group_scan(u, lam_re, lam_im, log_step, b_re, b_im, c_re, c_im):
    dt = jnp.exp(log_step)[:, None]
    mag = jnp.exp(lam_re * dt)
    ar = mag * jnp.cos(lam_im * dt)
    ai = mag * jnp.sin(lam_im * dt)
    den = lam_re * lam_re + lam_im * lam_im
    nr = ar - 1.0
    cr = (nr * lam_re + ai * lam_im) / den
    ci = (ai * lam_re - nr * lam_im) / den
    bbr = cr[..., None] * b_re - ci[..., None] * b_im
    bbi = cr[..., None] * b_im + ci[..., None] * b_re
    bur = jnp.einsum('blh,dph->dblp', u, bbr)
    bui = jnp.einsum('blh,dph->dblp', u, bbi)
    shape = bur.shape[1:]

    def run(d, reverse):
        a_r = jnp.broadcast_to(ar[d], shape)
        a_i = jnp.broadcast_to(ai[d], shape)
        _, _, sr, si = lax.associative_scan(_complex_combine, (a_r, a_i, bur[d], bui[d]),
                                            axis=1, reverse=reverse)
        return jnp.einsum('blp,hp->blh', sr, c_re[d]) - jnp.einsum('blp,hp->blh', si, c_im[d])

    return run(0, False) + run(1, True)


def s5_branch(ub, lam_re, lam_im, log_step, b_re, b_im, c_re, c_im, d_skip, w_glu_a, w_glu_b):
    B_, L_, _ = ub.shape
    uf = ub.astype(F32)
    ug = uf.reshape(B_, L_, SSM_GROUPS, SSM_GROUP).transpose(2, 0, 1, 3)
    yg = lax.map(lambda args: s5_group_scan(*args),
                 (ug, lam_re.astype(F32), lam_im.astype(F32), log_step.astype(F32),
                  b_re.astype(F32), b_im.astype(F32), c_re.astype(F32), c_im.astype(F32)))
    y = yg.transpose(1, 2, 0, 3).reshape(B_, L_, D_SSM) + d_skip.astype(F32) * uf
    y = jax.nn.gelu(y)
    return (y @ w_glu_a.astype(F32)) * jax.nn.sigmoid(y @ w_glu_b.astype(F32))


def peer_ffn(h, w_query, sub_keys, expert_u, expert_v):
    B_, L_, D = h.shape
    blocks = h.reshape(B_ * L_ // PEER_BLOCK, PEER_BLOCK, D)
    wq = w_query.astype(F32)
    keys = sub_keys.astype(F32)

    def one_block(xb):
        xf = xb.astype(F32)
        q = (xf @ wq).reshape(PEER_BLOCK, PEER_HEADS, 2, PEER_HALF)
        s = jnp.einsum('thcq,hckq->thck', q, keys)
        s_half, i_half = lax.top_k(s, PEER_TOPK)
        cand = s_half[:, :, 0, :, None] + s_half[:, :, 1, None, :]
        cand_idx = i_half[:, :, 0, :, None] * N_KEYS + i_half[:, :, 1, None, :]
        cand = cand.reshape(PEER_BLOCK, PEER_HEADS, PEER_TOPK * PEER_TOPK)
        cand_idx = cand_idx.reshape(PEER_BLOCK, PEER_HEADS, PEER_TOPK * PEER_TOPK)
        top_s, top_j = lax.top_k(cand, PEER_TOPK)
        idx = jnp.take_along_axis(cand_idx, top_j, axis=-1)
        gate = jax.nn.softmax(top_s, axis=-1)
        u_sel = expert_u[idx].astype(F32)
        act = jax.nn.gelu(jnp.einsum('td,thkd->thk', xf, u_sel))
        v_sel = expert_v[idx].astype(F32)
        return jnp.einsum('thk,thkd->td', gate * act, v_sel)

    return lax.map(one_block, blocks).reshape(B_, L_, D)


def hybrid_layer(x, g_mix, w_in, conv_w, conv_b, lru_wr, lru_br, lru_wi, lru_bi, lru_lam, w_lru_out,
                 s5_lam_re, s5_lam_im, s5_log_step, s5_b_re, s5_b_im, s5_c_re, s5_c_im, s5_d,
                 w_glu_a, w_glu_b, w_out, g_ffn, w_query, sub_keys, expert_u, expert_v):
    B_, L_, _ = x.shape
    h = rmsnorm(x, g_mix)
    z = h @ w_in.astype(h.dtype)
    xa = z[..., :D_LRU]
    ga = z[..., D_LRU:2 * D_LRU]
    ub = z[..., 2 * D_LRU:2 * D_LRU + D_SSM]
    gl = z[..., 2 * D_LRU + D_SSM:]
    ya = rglru_branch(xa, conv_w, conv_b, lru_wr, lru_br, lru_wi, lru_bi, lru_lam)
    ya = (ya * jax.nn.gelu(ga.astype(F32))) @ w_lru_out.astype(F32)
    yb = s5_branch(ub, s5_lam_re, s5_lam_im, s5_log_step, s5_b_re, s5_b_im, s5_c_re, s5_c_im,
                   s5_d, w_glu_a, w_glu_b)
    gates = jax.nn.sigmoid(gl.astype(F32)).reshape(B_, L_, 2, D_MODEL)
    merged = gates[:, :, 0, :] * ya + gates[:, :, 1, :] * yb
    x = x + (merged @ w_out.astype(F32)).astype(x.dtype)
    h2 = rmsnorm(x, g_ffn)
    x = x + peer_ffn(h2, w_query, sub_keys, expert_u, expert_v).astype(x.dtype)
    return x


def setup_inputs(seed: int = 0) -> dict:
    key = jax.random.key(seed)
    ks = jax.random.split(key, 32)
    nrm = lambda k, shape, scale: jax.random.normal(k, shape, F32) * scale
    a0 = jax.random.uniform(ks[8], (DEPTH, 2, D_LRU), F32, 0.9, 0.999)
    s0 = a0 ** (1.0 / LRU_C)
    lru_lam = jnp.log(s0) - jnp.log1p(-s0)
    n_idx = jnp.arange(SSM_STATE, dtype=F32) * math.pi
    return {
        'x_prompt': nrm(ks[0], (BATCH, SEQ, D_MODEL), 1.0),
        'x_sample': nrm(ks[1], (DEC_BATCH, DEC_SEQ, D_MODEL), 1.0),
        'g_mix': 1.0 + nrm(ks[2], (DEPTH, D_MODEL), 0.02),
        'w_in': nrm(ks[3], (DEPTH, D_MODEL, N_IN), D_MODEL ** -0.5),
        'conv_w': nrm(ks[4], (DEPTH, CONV_WIDTH, D_LRU), CONV_WIDTH ** -0.5),
        'conv_b': nrm(ks[5], (DEPTH, D_LRU), 0.01),
        'lru_wr': nrm(ks[6], (DEPTH, 2, LRU_HEADS, LRU_HEAD_DIM, LRU_HEAD_DIM), LRU_HEAD_DIM ** -0.5),
        'lru_br': nrm(ks[7], (DEPTH, 2, D_LRU), 0.01),
        'lru_wi': nrm(ks[9], (DEPTH, 2, LRU_HEADS, LRU_HEAD_DIM, LRU_HEAD_DIM), LRU_HEAD_DIM ** -0.5),
        'lru_bi': nrm(ks[10], (DEPTH, 2, D_LRU), 0.01),
        'lru_lam': lru_lam,
        'w_lru_out': nrm(ks[11], (DEPTH, D_LRU, D_MODEL), D_LRU ** -0.5),
        's5_lam_re': -0.5 + nrm(ks[12], (DEPTH, SSM_GROUPS, 2, SSM_STATE), 0.01),
        's5_lam_im': n_idx + nrm(ks[13], (DEPTH, SSM_GROUPS, 2, SSM_STATE), 0.01),
        's5_log_step': jax.random.uniform(ks[14], (DEPTH, SSM_GROUPS, 2), F32,
                                          math.log(1e-3), math.log(1e-1)),
        's5_b_re': nrm(ks[15], (DEPTH, SSM_GROUPS, 2, SSM_STATE, SSM_GROUP), (2 * SSM_GROUP) ** -0.5),
        's5_b_im': nrm(ks[16], (DEPTH, SSM_GROUPS, 2, SSM_STATE, SSM_GROUP), (2 * SSM_GROUP) ** -0.5),
        's5_c_re': nrm(ks[17], (DEPTH, SSM_GROUPS, 2, SSM_GROUP, SSM_STATE), (2 * SSM_STATE) ** -0.5),
        's5_c_im': nrm(ks[18], (DEPTH, SSM_GROUPS, 2, SSM_GROUP, SSM_STATE), (2 * SSM_STATE) ** -0.5),
        's5_d': nrm(ks[19], (DEPTH, D_SSM), 1.0),
        'w_glu_a': nrm(ks[20], (DEPTH, D_SSM, D_MODEL), D_SSM ** -0.5),
        'w_glu_b': nrm(ks[21], (DEPTH, D_SSM, D_MODEL), D_SSM ** -0.5),
        'w_out': nrm(ks[22], (DEPTH, D_MODEL, D_MODEL), D_MODEL ** -0.5),
        'g_ffn': 1.0 + nrm(ks[23], (DEPTH, D_MODEL), 0.02),
        'w_query': nrm(ks[24], (DEPTH, D_MODEL, PEER_HEADS * PEER_QDIM), D_MODEL ** -0.5),
        'sub_keys': nrm(ks[25], (DEPTH, PEER_HEADS, 2, N_KEYS, PEER_HALF), PEER_HALF ** -0.5),
        'expert_u': nrm(ks[26], (DEPTH, N_EXPERTS, D_MODEL), D_MODEL ** -0.5),
        'expert_v': nrm(ks[27], (DEPTH, N_EXPERTS, D_MODEL), (PEER_TOPK) ** -0.5),
        'g_final': 1.0 + nrm(ks[28], (D_MODEL,), 0.02),
    }


def reference(x_prompt, x_sample, g_mix, w_in, conv_w, conv_b, lru_wr, lru_br, lru_wi, lru_bi, lru_lam,
              w_lru_out, s5_lam_re, s5_lam_im, s5_log_step, s5_b_re, s5_b_im, s5_c_re, s5_c_im, s5_d,
              w_glu_a, w_glu_b, w_out, g_ffn, w_query, sub_keys, expert_u, expert_v, g_final):
    stack = (g_mix, w_in, conv_w, conv_b, lru_wr, lru_br, lru_wi, lru_bi, lru_lam, w_lru_out,
             s5_lam_re, s5_lam_im, s5_log_step, s5_b_re, s5_b_im, s5_c_re, s5_c_im, s5_d,
             w_glu_a, w_glu_b, w_out, g_ffn, w_query, sub_keys, expert_u, expert_v)
    xp = x_prompt
    xs = x_sample
    for l in range(DEPTH):
        layer_params = [p[l] for p in stack]
        xp = hybrid_layer(xp, *layer_params)
        xs = hybrid_layer(xs, *layer_params)
    y_prompt = rmsnorm(xp, g_final)
    y_sample = rmsnorm(xs, g_final)
    return (y_prompt, y_sample)
```

```python
import functools
import math

import jax
import jax.numpy as jnp
from jax import lax
from jax.experimental import pallas as pl
from jax.experimental.pallas import tpu as pltpu

F32 = jnp.float32
BF16 = jnp.bfloat16

LANES = 128
SUBLANES = 8
VMEM_LIMIT_BYTES = 56 * 1024 * 1024

D_MODEL = 1024
LRU_HEADS = 16
LRU_HEAD_DIM = D_MODEL // LRU_HEADS
LRU_C = 8.0
SSM_GROUP = 16
SSM_GROUPS = D_MODEL // SSM_GROUP
SSM_STATE = 64
N_KEYS = 128
PEER_HEADS = 8
PEER_TOPK = 16
PEER_HALF = 128
EPS = 1e-6

N_TILES = D_MODEL // LANES
GROUPS_PER_TILE = LANES // SSM_GROUP
S5_CHUNK = 8
S5_STATE_COLS = 2 * 2 * GROUPS_PER_TILE * SSM_STATE
HALF_STATE = GROUPS_PER_TILE * SSM_STATE


def _gelu(x):
    c = math.sqrt(2.0 / math.pi)
    return 0.5 * x * (1.0 + jnp.tanh(c * (x + 0.044715 * (x * x * x))))


def _rms(x, g):
    ms = jnp.mean(x * x, axis=-1, keepdims=True)
    return x * lax.rsqrt(ms + EPS) * g


def _params(*sem):
    return pltpu.CompilerParams(dimension_semantics=sem, vmem_limit_bytes=VMEM_LIMIT_BYTES)


def _in_proj_kernel(x_ref, g_ref, w_ref, z_ref):
    h = _rms(x_ref[...], g_ref[...])
    z_ref[...] = jnp.dot(h.astype(BF16), w_ref[...], preferred_element_type=F32)


def _in_proj(x2d, g_mix, w_in, tb):
    n = x2d.shape[0]
    nj = w_in.shape[1] // D_MODEL
    return pl.pallas_call(
        _in_proj_kernel,
        out_shape=jax.ShapeDtypeStruct((nj, n, D_MODEL), F32),
        grid=(nj, n // tb),
        in_specs=[pl.BlockSpec((tb, D_MODEL), lambda j, i: (i, 0)),
                  pl.BlockSpec((1, D_MODEL), lambda j, i: (0, 0)),
                  pl.BlockSpec((D_MODEL, D_MODEL), lambda j, i: (0, j))],
        out_specs=pl.BlockSpec((None, tb, D_MODEL), lambda j, i: (j, i, 0)),
        compiler_params=_params("arbitrary", "arbitrary"),
        name="in_proj",
    )(x2d, g_mix.reshape(1, D_MODEL), w_in)


def _group_scan(a, b, reverse):
    row = lax.broadcasted_iota(jnp.int32, a.shape, 0) & (SUBLANES - 1)
    n = a.shape[0]
    for s in (1, 2, 4):
        if reverse:
            a_s = pltpu.roll(a, n - s, 0)
            b_s = pltpu.roll(b, n - s, 0)
            valid = row < SUBLANES - s
        else:
            a_s = pltpu.roll(a, s, 0)
            b_s = pltpu.roll(b, s, 0)
            valid = row >= s
        b = a * jnp.where(valid, b_s, 0.0) + b
        a = a * jnp.where(valid, a_s, 1.0)
    return a, b


def _carry_groups(a, b, h, reverse):
    groups = a.shape[0] // SUBLANES
    outs = [None] * groups
    order = range(groups - 1, -1, -1) if reverse else range(groups)
    edge = 0 if reverse else SUBLANES - 1
    for k in order:
        ak = a[k * SUBLANES:(k + 1) * SUBLANES]
        bk = b[k * SUBLANES:(k + 1) * SUBLANES]
        outs[k] = ak * h + bk
        h = ak[edge:edge + 1] * h + bk[edge:edge + 1]
    return jnp.concatenate(outs, axis=0), h


def _lru_kernel(xa_ref, ga_ref, cw_ref, cb_ref, wgf_ref, wgb_ref, bgf_ref, bgb_ref, sp_ref,
                o_ref, hf_ref, hb_ref, *, rt):
    seq = xa_ref.shape[0]
    nt = seq // rt
    cw = cw_ref[...]
    cb = cb_ref[...]

    def conv_tile(j):
        r0 = pl.multiple_of(j * rt, rt)
        cur = xa_ref[pl.ds(r0, rt), :]
        p0 = pl.multiple_of(jnp.maximum(r0 - SUBLANES, 0), SUBLANES)
        n0 = pl.multiple_of(jnp.minimum(r0 + rt, seq - SUBLANES), SUBLANES)
        prev = jnp.where(j > 0, xa_ref[pl.ds(p0, SUBLANES), :], 0.0)
        nxt = jnp.where(j < nt - 1, xa_ref[pl.ds(n0, SUBLANES), :], 0.0)
        ext = jnp.concatenate([prev, cur, nxt], axis=0)
        n = rt + 2 * SUBLANES
        xm2 = pltpu.roll(ext, 2, 0)[SUBLANES:SUBLANES + rt]
        xm1 = pltpu.roll(ext, 1, 0)[SUBLANES:SUBLANES + rt]
        xp1 = pltpu.roll(ext, n - 1, 0)[SUBLANES:SUBLANES + rt]
        return cw[0:1] * xm2 + cw[1:2] * xm1 + cw[2:3] * cur + cw[3:4] * xp1 + cb

    def direction(j, h, wg_ref, bg_ref, sp, dst_ref, reverse):
        xc = conv_tile(j)
        gz = jnp.dot(xc.astype(BF16), wg_ref[...], preferred_element_type=F32) + bg_ref[...]
        r = jax.nn.sigmoid(gz[:, :LANES])
        i = jax.nn.sigmoid(gz[:, LANES:])
        log_a = -LRU_C * r * sp
        a = jnp.exp(log_a)
        b = jnp.sqrt(-jnp.tanh(log_a) * (1.0 + a * a)) * i * xc
        a_g, b_g = _group_scan(a, b, reverse)
        h_tile, h = _carry_groups(a_g, b_g, h, reverse)
        dst_ref[pl.ds(pl.multiple_of(j * rt, rt), rt), :] = h_tile
        return h

    def body(j, carry):
        hf, hb = carry
        hf = direction(j, hf, wgf_ref, bgf_ref, sp_ref[0:1, :], hf_ref, False)
        hb = direction(nt - 1 - j, hb, wgb_ref, bgb_ref, sp_ref[1:2, :], hb_ref, True)
        return hf, hb

    zero = jnp.zeros((1, LANES), F32)
    lax.fori_loop(0, nt, body, (zero, zero))

    def finish(j, c):
        rows = pl.ds(pl.multiple_of(j * rt, rt), rt)
        y = (hf_ref[rows, :] + hb_ref[rows, :]) * _gelu(ga_ref[rows, :])
        o_ref[rows, :] = y.astype(o_ref.dtype)
        return c

    lax.fori_loop(0, nt, finish, 0)


def _lru(z5, conv_w, conv_b, wgf, wgb, bgf, bgb, sp, rt):
    _, nb, seq, _ = z5.shape
    slab = lambda k: pl.BlockSpec((None, None, seq, LANES), lambda b, o: (k, b, 0, o))
    per_tile = lambda shape: pl.BlockSpec((None,) + shape, lambda b, o: (o,) + (0,) * len(shape))
    return pl.pallas_call(
        functools.partial(_lru_kernel, rt=rt),
        out_shape=jax.ShapeDtypeStruct((nb, seq, D_MODEL), BF16),
        grid=(nb, N_TILES),
        in_specs=[slab(0), slab(1),
                  pl.BlockSpec((4, LANES), lambda b, o: (0, o)),
                  pl.BlockSpec((1, LANES), lambda b, o: (0, o)),
                  per_tile((LANES, 2 * LANES)), per_tile((LANES, 2 * LANES)),
                  per_tile((1, 2 * LANES)), per_tile((1, 2 * LANES)),
                  pl.BlockSpec((2, LANES), lambda b, o: (0, o))],
        out_specs=pl.BlockSpec((None, seq, LANES), lambda b, o: (b, 0, o)),
        scratch_shapes=[pltpu.VMEM((seq, LANES), F32), pltpu.VMEM((seq, LANES), F32)],
        compiler_params=_params("arbitrary", "arbitrary"),
        name="lru",
    )(z5, z5, conv_w, conv_b, wgf, wgb, bgf, bgb, sp)


def _lru_weights(lru_wr, lru_br, lru_wi, lru_bi, lru_lam):
    eye = jnp.eye(2, dtype=F32)

    def blockdiag(w):
        w = w.reshape(N_TILES, 2, LRU_HEAD_DIM, LRU_HEAD_DIM)
        return jnp.einsum('ohij,hk->ohikj', w, eye).reshape(N_TILES, LANES, LANES)

    def direction(d):
        wg = jnp.concatenate([blockdiag(lru_wr[d]), blockdiag(lru_wi[d])], axis=-1).astype(BF16)
        bg = jnp.concatenate([lru_br[d].reshape(N_TILES, 1, LANES),
                              lru_bi[d].reshape(N_TILES, 1, LANES)], axis=-1)
        return wg, bg

    wgf, bgf = direction(0)
    wgb, bgb = direction(1)
    return wgf, wgb, bgf, bgb, jax.nn.softplus(-lru_lam)


def _cmul(ar, ai, br, bi):
    return ar * br - ai * bi, ar * bi + ai * br


def _s5_kernel(u_ref, wi_ref, wp_ref, wq_ref, tab_ref, o_ref, xcat_ref, st_ref, *, rb):
    seq = u_ref.shape[0]
    nc = seq // S5_CHUNK
    nblk = nc // rb
    ntile = nc // SUBLANES

    for p in range(S5_CHUNK):
        xcat_ref[:, p * LANES:(p + 1) * LANES] = u_ref[pl.ds(p, nc, stride=S5_CHUNK), :].astype(BF16)

    for k in range(nblk):
        rows = pl.ds(k * rb, rb)
        st_ref[rows, :] = jnp.dot(xcat_ref[rows, :], wp_ref[...], preferred_element_type=F32)

    row = lax.broadcasted_iota(jnp.int32, (SUBLANES, HALF_STATE), 0)

    def tile_scan(t, er, ei, d, reverse):
        rows = pl.ds(pl.multiple_of(t * SUBLANES, SUBLANES), SUBLANES)
        c0 = d * 2 * HALF_STATE
        xr = st_ref[rows, c0:c0 + HALF_STATE]
        xi = st_ref[rows, c0 + HALF_STATE:c0 + 2 * HALF_STATE]
        for m, s in enumerate((1, 2, 4)):
            shift = SUBLANES - s if reverse else s
            valid = (row < SUBLANES - s) if reverse else (row >= s)
            sr = jnp.where(valid, pltpu.roll(xr, shift, 0), 0.0)
            si = jnp.where(valid, pltpu.roll(xi, shift, 0), 0.0)
            pr = tab_ref[d, 0, m:m + 1, :]
            pi = tab_ref[d, 1, m:m + 1, :]
            mr, mi = _cmul(pr, pi, sr, si)
            xr = xr + mr
            xi = xi + mi
        shift = SUBLANES - 1 if reverse else 1
        valid = (row < SUBLANES - 1) if reverse else (row >= 1)
        qr = jnp.where(valid, pltpu.roll(xr, shift, 0), 0.0)
        qi = jnp.where(valid, pltpu.roll(xi, shift, 0), 0.0)
        wr = tab_ref[d, 0, 4:4 + SUBLANES, :]
        wi = tab_ref[d, 1, 4:4 + SUBLANES, :]
        cr, ci = _cmul(wr, wi, er, ei)
        st_ref[rows, c0:c0 + HALF_STATE] = qr + cr
        st_ref[rows, c0 + HALF_STATE:c0 + 2 * HALF_STATE] = qi + ci
        edge = 0 if reverse else SUBLANES - 1
        nr, ni = _cmul(tab_ref[d, 0, 3:4, :], tab_ref[d, 1, 3:4, :], er, ei)
        return nr + xr[edge:edge + 1], ni + xi[edge:edge + 1]

    def body(t, carry):
        fr, fi, br, bi = carry
        fr, fi = tile_scan(t, fr, fi, 0, False)
        br, bi = tile_scan(ntile - 1 - t, br, bi, 1, True)
        return fr, fi, br, bi

    zero = jnp.zeros((1, HALF_STATE), F32)
    lax.fori_loop(0, ntile, body, (zero, zero, zero, zero))

    for k in range(nblk):
        rows = pl.ds(k * rb, rb)
        y = jnp.dot(xcat_ref[rows, :], wi_ref[...], preferred_element_type=F32)
        y = y + jnp.dot(st_ref[rows, :].astype(BF16), wq_ref[...], preferred_element_type=F32)
        for p in range(S5_CHUNK):
            o_ref[pl.ds(k * rb * S5_CHUNK + p, rb, stride=S5_CHUNK), :] = y[:, p * LANES:(p + 1) * LANES]


def _s5(z5, wi, wp, wq, tab, rb):
    _, nb, seq, _ = z5.shape
    nc = seq // S5_CHUNK
    kdim = S5_CHUNK * LANES
    per_tile = lambda shape: pl.BlockSpec((None,) + shape, lambda o, b: (o,) + (0,) * len(shape))
    return pl.pallas_call(
        functools.partial(_s5_kernel, rb=rb),
        out_shape=jax.ShapeDtypeStruct((nb, seq, D_MODEL), F32),
        grid=(N_TILES, nb),
        in_specs=[pl.BlockSpec((None, None, seq, LANES), lambda o, b: (2, b, 0, o)),
                  per_tile((kdim, kdim)), per_tile((kdim, S5_STATE_COLS)),
                  per_tile((S5_STATE_COLS, kdim)), per_tile((2, 2, 16, HALF_STATE))],
        out_specs=pl.BlockSpec((None, seq, LANES), lambda o, b: (b, 0, o)),
        scratch_shapes=[pltpu.VMEM((nc, kdim), BF16), pltpu.VMEM((nc, S5_STATE_COLS), F32)],
        compiler_params=_params("arbitrary", "arbitrary"),
        name="s5",
    )(z5, wi, wp, wq, tab)


def _s5_weights(lam_re, lam_im, log_step, b_re, b_im, c_re, c_im):
    T = S5_CHUNK
    G, P, H = SSM_GROUPS, SSM_STATE, SSM_GROUP
    dt = jnp.exp(log_step)[:, :, None, None]
    kk = jnp.arange(8 * T + 1, dtype=F32)[None, None, :, None]
    mag = jnp.exp(lam_re[:, :, None, :] * dt * kk)
    ang = lam_im[:, :, None, :] * dt * kk
    pr = mag * jnp.cos(ang)
    pi = mag * jnp.sin(ang)
    ar, ai = pr[:, :, 1], pi[:, :, 1]
    den = lam_re * lam_re + lam_im * lam_im
    nr = ar - 1.0
    cr = (nr * lam_re + ai * lam_im) / den
    ci = (ai * lam_re - nr * lam_im) / den
    bbr = cr[..., None] * b_re - ci[..., None] * b_im
    bbi = cr[..., None] * b_im + ci[..., None] * b_re
    mr = pr[:, :, :T, :, None] * bbr[:, :, None] - pi[:, :, :T, :, None] * bbi[:, :, None]
    mi = pr[:, :, :T, :, None] * bbi[:, :, None] + pi[:, :, :T, :, None] * bbr[:, :, None]
    taps = jnp.einsum('gdon,gdknh->gdkoh', c_re, mr) - jnp.einsum('gdon,gdknh->gdkoh', c_im, mi)
    pos = jnp.arange(T)
    lag = pos[None, :] - pos[:, None]
    kf = taps[:, 0][:, jnp.clip(lag, 0, T - 1)] * (lag >= 0)[None, :, :, None, None]
    kb = taps[:, 1][:, jnp.clip(-lag, 0, T - 1)] * (lag <= 0)[None, :, :, None, None]
    intra = (kf + kb).transpose(0, 1, 4, 2, 3)
    eye = jnp.eye(GROUPS_PER_TILE, dtype=F32)
    intra = intra.reshape(N_TILES, GROUPS_PER_TILE, T, H, T, H)
    wi = jnp.einsum('ogphqk,gj->opghqjk', intra, eye).reshape(N_TILES, T * LANES, T * LANES)

    def pmat(m):
        f = m[:, 0, ::-1]
        b = m[:, 1]
        return jnp.stack([f, b], axis=1)
    pm = jnp.stack([pmat(mr), pmat(mi)], axis=2)
    pm = pm.transpose(0, 3, 5, 1, 2, 4).reshape(N_TILES, GROUPS_PER_TILE, T, H, 2, 2, P)
    wp = jnp.einsum('ogphdcn,gj->opghdcjn', pm, eye).reshape(N_TILES, T * LANES, S5_STATE_COLS)

    lag_f = pos + 1
    lag_b = T - pos
    def qpair(d, lags):
        prd = pr[:, d][:, lags]
        pid = pi[:, d][:, lags]
        on_re = c_re[:, d][:, None] * prd[:, :, None, :] - c_im[:, d][:, None] * pid[:, :, None, :]
        on_im = -(c_re[:, d][:, None] * pid[:, :, None, :] + c_im[:, d][:, None] * prd[:, :, None, :])
        return jnp.stack([on_re, on_im], axis=1)
    qm = jnp.stack([qpair(0, lag_f), qpair(1, lag_b)], axis=1)
    qm = qm.transpose(0, 1, 2, 5, 3, 4).reshape(N_TILES, GROUPS_PER_TILE, 2, 2, P, T, H)
    wq = jnp.einsum('ogdcnph,gj->odcgnpjh', qm, eye).reshape(N_TILES, S5_STATE_COLS, T * LANES)

    def lanes(x):
        x = x.reshape(N_TILES, GROUPS_PER_TILE, 2, x.shape[2], P)
        return x.transpose(0, 2, 3, 1, 4).reshape(N_TILES, 2, x.shape[3], HALF_STATE)
    steps = jnp.array([T, 2 * T, 4 * T, 8 * T])
    rows_f = T * jnp.arange(SUBLANES)
    rows_b = T * (SUBLANES - 1 - jnp.arange(SUBLANES))
    def table(p):
        head = p[:, :, steps]
        tail = jnp.stack([p[:, 0][:, rows_f], p[:, 1][:, rows_b]], axis=1)
        full = jnp.concatenate([head, tail, jnp.zeros_like(head)], axis=2)
        return lanes(full)
    tab = jnp.stack([table(pr), table(pi)], axis=2)
    return wi.astype(BF16), wp.astype(BF16), wq.astype(BF16), tab


def _merge_kernel(x_ref, ya_ref, ys_ref, ub_ref, g1_ref, g2_ref, wlo_ref, wga_ref, wgb_ref, wo_ref,
                  d_ref, gf_ref, x1_ref, h2t_ref):
    ya = jnp.dot(ya_ref[...], wlo_ref[...], preferred_element_type=F32)
    yg = _gelu(ys_ref[...] + d_ref[...] * ub_ref[...]).astype(BF16)
    yb = (jnp.dot(yg, wga_ref[...], preferred_element_type=F32)
          * jax.nn.sigmoid(jnp.dot(yg, wgb_ref[...], preferred_element_type=F32)))
    merged = jax.nn.sigmoid(g1_ref[...]) * ya + jax.nn.sigmoid(g2_ref[...]) * yb
    x1 = x_ref[...] + jnp.dot(merged.astype(BF16), wo_ref[...], preferred_element_type=F32)
    x1_ref[...] = x1
    h2t_ref[...] = _rms(x1, gf_ref[...]).T.astype(BF16)


def _merge(x2d, ya, ys, z5f, w_lru_out, w_glu_a, w_glu_b, w_out, s5_d, g_ffn, tb):
    n = x2d.shape[0]
    tok = pl.BlockSpec((tb, D_MODEL), lambda i: (i, 0))
    zsl = lambda k: pl.BlockSpec((None, tb, D_MODEL), lambda i: (k, i, 0))
    wsp = pl.BlockSpec((D_MODEL, D_MODEL), lambda i: (0, 0))
    vec = pl.BlockSpec((1, D_MODEL), lambda i: (0, 0))
    return pl.pallas_call(
        _merge_kernel,
        out_shape=(jax.ShapeDtypeStruct((n, D_MODEL), F32), jax.ShapeDtypeStruct((D_MODEL, n), BF16)),
        grid=(n // tb,),
        in_specs=[tok, tok, tok, zsl(2), zsl(3), zsl(4), wsp, wsp, wsp, wsp, vec, vec],
        out_specs=(tok, pl.BlockSpec((D_MODEL, tb), lambda i: (0, i))),
        compiler_params=_params("arbitrary"),
        name="merge",
    )(x2d, ya, ys, z5f, z5f, z5f, w_lru_out, w_glu_a, w_glu_b, w_out,
      s5_d.reshape(1, D_MODEL), g_ffn.reshape(1, D_MODEL))


def _top16(s):
    keys = lax.broadcasted_iota(jnp.int32, s.shape, 0)
    slot = lax.broadcasted_iota(jnp.int32, (PEER_TOPK, s.shape[1]), 0)

    def body(k, carry):
        s, rank, vals = carry
        m = jnp.max(s, axis=0, keepdims=True)
        first = jnp.min(jnp.where(s == m, keys, N_KEYS), axis=0, keepdims=True)
        sel = keys == first
        s = jnp.where(sel, -jnp.inf, s)
        rank = jnp.where(sel, k, rank)
        vals = jnp.where(slot == k, m, vals)
        return s, rank, vals

    init = (s, jnp.full(s.shape, PEER_TOPK, jnp.int32), jnp.zeros((PEER_TOPK, s.shape[1]), F32))
    _, rank, vals = lax.fori_loop(0, PEER_TOPK, body, init)
    return vals, rank


def _pair_select(v1, v2):
    w = v1.shape[1]
    r8 = lax.broadcasted_iota(jnp.int32, (SUBLANES, w), 0)
    neg = -jnp.inf
    tiles = []

    big = PEER_TOPK * PEER_TOPK

    def fixed_a(a, b0, limit):
        sums = v1[a:a + 1] + v2[b0:b0 + SUBLANES]
        b = r8 + b0
        ok = b < limit
        tiles.append((jnp.where(ok, sums, neg), jnp.where(ok, a * PEER_TOPK + b, big), a, None))

    def fixed_b(b, a0, lo, hi):
        sums = v1[a0:a0 + SUBLANES] + v2[b:b + 1]
        a = r8 + a0
        ok = jnp.where(a >= lo, a, hi) < hi
        tiles.append((jnp.where(ok, sums, neg), jnp.where(ok, a * PEER_TOPK + b, big), None, a0))

    fixed_a(0, 0, 16); fixed_a(0, 8, 16); fixed_a(1, 0, 8); fixed_a(2, 0, 5); fixed_a(3, 0, 4)
    fixed_b(0, 0, 4, 16); fixed_b(0, 8, 4, 16); fixed_b(1, 0, 4, 8); fixed_b(2, 0, 4, 5)

    top = v1[0:1] + v2[0:1]
    sums = [t[0] for t in tiles]
    picked = [jnp.zeros((SUBLANES, w), F32) for _ in tiles]
    for _ in range(PEER_TOPK):
        m = functools.reduce(jnp.maximum, sums)
        m = jnp.max(m, axis=0, keepdims=True)
        cand = [jnp.where(s == m, t[1], big) for s, t in zip(sums, tiles)]
        first = jnp.min(functools.reduce(jnp.minimum, cand), axis=0, keepdims=True)
        for n, t in enumerate(tiles):
            sel = t[1] == first
            picked[n] = jnp.where(sel, 1.0, picked[n])
            sums[n] = jnp.where(sel, neg, sums[n])

    r16 = lax.broadcasted_iota(jnp.int32, (PEER_TOPK, w), 0)
    cnt = jnp.zeros((PEER_TOPK, w), F32)
    den = jnp.zeros((SUBLANES, w), F32)
    lo_rows = jnp.zeros((SUBLANES, w), F32)
    hi_rows = jnp.zeros((SUBLANES, w), F32)
    for n, t in enumerate(tiles):
        den = den + picked[n] * jnp.exp(jnp.where(picked[n] > 0, t[0], top) - top)
        if t[2] is not None:
            cnt = cnt + jnp.where(r16 == t[2], jnp.sum(picked[n], axis=0, keepdims=True), 0.0)
        elif t[3] == 0:
            lo_rows = lo_rows + picked[n]
        else:
            hi_rows = hi_rows + picked[n]
    cnt = cnt + jnp.concatenate([lo_rows, hi_rows], axis=0)
    return cnt, jnp.sum(den, axis=0, keepdims=True)


def _route_kernel(h2t_ref, wqt_ref, keys_ref, p_ref, c_ref, q_ref, r2_ref):
    qt = jnp.dot(wqt_ref[...], h2t_ref[...], preferred_element_type=F32)
    s1 = jnp.dot(keys_ref[0], qt[:PEER_HALF].astype(BF16), preferred_element_type=F32)
    s2 = jnp.dot(keys_ref[1], qt[PEER_HALF:].astype(BF16), preferred_element_type=F32)
    for lt in range(s1.shape[1] // LANES):
        cols = slice(lt * LANES, (lt + 1) * LANES)
        a1, a2 = s1[:, cols], s2[:, cols]
        v1, rank1 = _top16(a1)
        v2, rank2 = _top16(a2)
        cnt, den = _pair_select(v1, v2)
        in1 = rank1 < PEER_TOPK
        in2 = rank2 < PEER_TOPK
        ck = jnp.zeros(a1.shape, F32)
        for a in range(PEER_TOPK):
            ck = jnp.where(rank1 == a, cnt[a:a + 1], ck)
        p_ref[:, cols] = jnp.where(in1, jnp.exp(jnp.where(in1, a1, v1[0:1]) - v1[0:1]), 0.0)
        c_ref[:, cols] = ck
        q_ref[:, cols] = jnp.where(in2, jnp.exp(jnp.where(in2, a2, v2[0:1]) - v2[0:1]) / den, 0.0)
        r2_ref[:, cols] = rank2.astype(F32)


def _route(h2t, wqt, keys, tb):
    n = h2t.shape[1]
    out = jax.ShapeDtypeStruct((PEER_HEADS, N_KEYS, n), F32)
    osp = pl.BlockSpec((None, N_KEYS, tb), lambda i, h: (h, 0, i))
    return pl.pallas_call(
        _route_kernel,
        out_shape=(out, out, out, out),
        grid=(n // tb, PEER_HEADS),
        in_specs=[pl.BlockSpec((D_MODEL, tb), lambda i, h: (0, i)),
                  pl.BlockSpec((2 * PEER_HALF, D_MODEL), lambda i, h: (h, 0)),
                  pl.BlockSpec((None, 2, N_KEYS, PEER_HALF), lambda i, h: (h, 0, 0, 0))],
        out_specs=(osp, osp, osp, osp),
        compiler_params=_params("arbitrary", "arbitrary"),
        name="route",
    )(h2t, wqt, keys)


def _peer_kernel(h2t_ref, u_ref, vt_ref, p_ref, c_ref, q_ref, r2_ref, x1_ref, g_ref, y_ref, acc_ref,
                 *, rows_per_step):
    ib = pl.program_id(1)

    @pl.when(ib == 0)
    def _():
        acc_ref[...] = jnp.zeros_like(acc_ref)

    act = jnp.dot(u_ref[...], h2t_ref[...], preferred_element_type=F32)
    parts = []
    for ii in range(rows_per_step):
        gate = jnp.zeros((N_KEYS, act.shape[1]), F32)
        for h in range(PEER_HEADS):
            hit = r2_ref[h] < c_ref[h, ii:ii + 1, :]
            gate = gate + jnp.where(hit, q_ref[h], 0.0) * p_ref[h, ii:ii + 1, :]
        a = act[ii * N_KEYS:(ii + 1) * N_KEYS]
        parts.append((gate * _gelu(a)).astype(BF16))
    wt = jnp.concatenate(parts, axis=0)
    acc_ref[...] += jnp.dot(vt_ref[...], wt, preferred_element_type=F32)

    @pl.when(ib == pl.num_programs(1) - 1)
    def _():
        x2 = x1_ref[...] + acc_ref[...].T
        y_ref[...] = _rms(x2, g_ref[...])


def _peer(h2t, u, vt, pk, ck, qk, r2k, x1, g_final, tb, rows_per_step):
    n = h2t.shape[1]
    eb = rows_per_step * N_KEYS
    small = pl.BlockSpec((PEER_HEADS, rows_per_step, tb), lambda i, e: (0, e, i))
    full = pl.BlockSpec((PEER_HEADS, N_KEYS, tb), lambda i, e: (0, 0, i))
    tok = pl.BlockSpec((tb, D_MODEL), lambda i, e: (i, 0))
    return pl.pallas_call(
        functools.partial(_peer_kernel, rows_per_step=rows_per_step),
        out_shape=jax.ShapeDtypeStruct((n, D_MODEL), F32),
        grid=(n // tb, N_KEYS // rows_per_step),
        in_specs=[pl.BlockSpec((D_MODEL, tb), lambda i, e: (0, i)),
                  pl.BlockSpec((eb, D_MODEL), lambda i, e: (e, 0)),
                  pl.BlockSpec((D_MODEL, eb), lambda i, e: (0, e)),
                  small, small, full, full, tok,
                  pl.BlockSpec((1, D_MODEL), lambda i, e: (0, 0))],
        out_specs=tok,
        scratch_shapes=[pltpu.VMEM((D_MODEL, tb), F32)],
        compiler_params=_params("arbitrary", "arbitrary"),
        name="peer",
    )(h2t, u, vt, pk, ck, qk, r2k, x1, g_final.reshape(1, D_MODEL))


def _layer(x, prm):
    nb, seq, _ = x.shape
    n = nb * seq
    x2d = x.reshape(n, D_MODEL)
    z5 = _in_proj(x2d, prm['g_mix'], prm['w_in'], 512)
    z5s = z5.reshape(z5.shape[0], nb, seq, D_MODEL)
    ya = _lru(z5s, prm['conv_w'], prm['conv_b'], *prm['lru'], rt=256)
    ys = _s5(z5s, *prm['s5'], rb=min(256, seq // S5_CHUNK))
    x1, h2t = _merge(x2d, ya.reshape(n, D_MODEL), ys.reshape(n, D_MODEL), z5,
                     prm['w_lru_out'], prm['w_glu_a'], prm['w_glu_b'], prm['w_out'],
                     prm['s5_d'], prm['g_ffn'], 256)
    pk, ck, qk, r2k = _route(h2t, prm['wqt'], prm['keys'], 512)
    y = _peer(h2t, prm['u'], prm['vt'], pk, ck, qk, r2k, x1, prm['g_final'], 512, 8)
    return y.reshape(nb, seq, D_MODEL)


def kernel(x_prompt, x_sample, g_mix, w_in, conv_w, conv_b, lru_wr, lru_br, lru_wi, lru_bi, lru_lam,
           w_lru_out, s5_lam_re, s5_lam_im, s5_log_step, s5_b_re, s5_b_im, s5_c_re, s5_c_im, s5_d,
           w_glu_a, w_glu_b, w_out, g_ffn, w_query, sub_keys, expert_u, expert_v, g_final):
    depth = g_mix.shape[0]
    xp, xs = x_prompt, x_sample
    for l in range(depth):
        prm = {
            'g_mix': g_mix[l], 'w_in': w_in[l].astype(BF16),
            'conv_w': conv_w[l], 'conv_b': conv_b[l].reshape(1, D_MODEL),
            'lru': _lru_weights(lru_wr[l], lru_br[l], lru_wi[l], lru_bi[l], lru_lam[l]),
            's5': _s5_weights(s5_lam_re[l], s5_lam_im[l], s5_log_step[l], s5_b_re[l], s5_b_im[l],
                              s5_c_re[l], s5_c_im[l]),
            'w_lru_out': w_lru_out[l].astype(BF16), 'w_glu_a': w_glu_a[l].astype(BF16),
            'w_glu_b': w_glu_b[l].astype(BF16), 'w_out': w_out[l].astype(BF16),
            's5_d': s5_d[l], 'g_ffn': g_ffn[l],
            'wqt': w_query[l].T.astype(BF16),
            'keys': sub_keys[l].astype(BF16),
            'u': expert_u[l].astype(BF16), 'vt': expert_v[l].T.astype(BF16),
            'g_final': g_final,
        }
        assert depth == 1
        xp = _layer(xp, prm)
        xs = _layer(xs, prm)
    return (xp, xs)
```

```python
import functools
import math

import jax
import jax.numpy as jnp
from jax import lax
from jax.experimental import pallas as pl
from jax.experimental.pallas import tpu as pltpu

F32 = jnp.float32
BF16 = jnp.bfloat16

LANES = 128
SUBLANES = 8
VMEM_LIMIT_BYTES = 56 * 1024 * 1024

D_MODEL = 1024
LRU_HEADS = 16
LRU_HEAD_DIM = D_MODEL // LRU_HEADS
LRU_C = 8.0
SSM_GROUP = 16
SSM_GROUPS = D_MODEL // SSM_GROUP
SSM_STATE = 64
N_KEYS = 128
PEER_HEADS = 8
PEER_TOPK = 16
PEER_HALF = 128
EPS = 1e-6

N_TILES = D_MODEL // LANES
GROUPS_PER_TILE = LANES // SSM_GROUP
S5_CHUNK = 8
S5_STATE_COLS = 2 * 2 * GROUPS_PER_TILE * SSM_STATE
HALF_STATE = GROUPS_PER_TILE * SSM_STATE


def _gelu(x):
    c = math.sqrt(2.0 / math.pi)
    return 0.5 * x * (1.0 + jnp.tanh(c * (x + 0.044715 * (x * x * x))))


def _rms(x, g):
    ms = jnp.mean(x * x, axis=-1, keepdims=True)
    return x * lax.rsqrt(ms + EPS) * g


def _params(*sem):
    return pltpu.CompilerParams(dimension_semantics=sem, vmem_limit_bytes=VMEM_LIMIT_BYTES)


def _in_proj_kernel(x_ref, g_ref, w_ref, z_ref):
    h = _rms(x_ref[...], g_ref[...])
    z_ref[...] = jnp.dot(h.astype(BF16), w_ref[...], preferred_element_type=F32)


def _in_proj(x2d, g_mix, w_in, tb):
    n = x2d.shape[0]
    nj = w_in.shape[1] // D_MODEL
    return pl.pallas_call(
        _in_proj_kernel,
        out_shape=jax.ShapeDtypeStruct((nj, n, D_MODEL), F32),
        grid=(nj, n // tb),
        in_specs=[pl.BlockSpec((tb, D_MODEL), lambda j, i: (i, 0)),
                  pl.BlockSpec((1, D_MODEL), lambda j, i: (0, 0)),
                  pl.BlockSpec((D_MODEL, D_MODEL), lambda j, i: (0, j))],
        out_specs=pl.BlockSpec((None, tb, D_MODEL), lambda j, i: (j, i, 0)),
        compiler_params=_params("arbitrary", "arbitrary"),
        name="in_proj",
    )(x2d, g_mix.reshape(1, D_MODEL), w_in)


def _group_scan(a, b, reverse):
    row = lax.broadcasted_iota(jnp.int32, a.shape, 0) & (SUBLANES - 1)
    n = a.shape[0]
    for s in (1, 2, 4):
        if reverse:
            a_s = pltpu.roll(a, n - s, 0)
            b_s = pltpu.roll(b, n - s, 0)
            valid = row < SUBLANES - s
        else:
            a_s = pltpu.roll(a, s, 0)
            b_s = pltpu.roll(b, s, 0)
            valid = row >= s
        b = a * jnp.where(valid, b_s, 0.0) + b
        a = a * jnp.where(valid, a_s, 1.0)
    return a, b


def _carry_groups(a, b, h, reverse):
    groups = a.shape[0] // SUBLANES
    outs = [None] * groups
    order = range(groups - 1, -1, -1) if reverse else range(groups)
    edge = 0 if reverse else SUBLANES - 1
    for k in order:
        ak = a[k * SUBLANES:(k + 1) * SUBLANES]
        bk = b[k * SUBLANES:(k + 1) * SUBLANES]
        outs[k] = ak * h + bk
        h = ak[edge:edge + 1] * h + bk[edge:edge + 1]
    return jnp.concatenate(outs, axis=0), h


def _lru_kernel(xa_ref, ga_ref, cw_ref, cb_ref, wgf_ref, wgb_ref, bgf_ref, bgb_ref, sp_ref,
                o_ref, hf_ref, hb_ref, *, rt):
    seq = xa_ref.shape[0]
    nt = seq // rt
    cw = cw_ref[...]
    cb = cb_ref[...]

    def conv_tile(j):
        r0 = pl.multiple_of(j * rt, rt)
        cur = xa_ref[pl.ds(r0, rt), :]
        p0 = pl.multiple_of(jnp.maximum(r0 - SUBLANES, 0), SUBLANES)
        n0 = pl.multiple_of(jnp.minimum(r0 + rt, seq - SUBLANES), SUBLANES)
        prev = jnp.where(j > 0, xa_ref[pl.ds(p0, SUBLANES), :], 0.0)
        nxt = jnp.where(j < nt - 1, xa_ref[pl.ds(n0, SUBLANES), :], 0.0)
        ext = jnp.concatenate([prev, cur, nxt], axis=0)
        n = rt + 2 * SUBLANES
        xm2 = pltpu.roll(ext, 2, 0)[SUBLANES:SUBLANES + rt]
        xm1 = pltpu.roll(ext, 1, 0)[SUBLANES:SUBLANES + rt]
        xp1 = pltpu.roll(ext, n - 1, 0)[SUBLANES:SUBLANES + rt]
        return cw[0:1] * xm2 + cw[1:2] * xm1 + cw[2:3] * cur + cw[3:4] * xp1 + cb

    def direction(j, h, wg_ref, bg_ref, sp, dst_ref, reverse):
        xc = conv_tile(j)
        gz = jnp.dot(xc.astype(BF16), wg_ref[...], preferred_element_type=F32) + bg_ref[...]
        r = jax.nn.sigmoid(gz[:, :LANES])
        i = jax.nn.sigmoid(gz[:, LANES:])
        log_a = -LRU_C * r * sp
        a = jnp.exp(log_a)
        b = jnp.sqrt(-jnp.tanh(log_a) * (1.0 + a * a)) * i * xc
        a_g, b_g = _group_scan(a, b, reverse)
        h_tile, h = _carry_groups(a_g, b_g, h, reverse)
        dst_ref[pl.ds(pl.multiple_of(j * rt, rt), rt), :] = h_tile
        return h

    def body(j, carry):
        hf, hb = carry
        hf = direction(j, hf, wgf_ref, bgf_ref, sp_ref[0:1, :], hf_ref, False)
        hb = direction(nt - 1 - j, hb, wgb_ref, bgb_ref, sp_ref[1:2, :], hb_ref, True)
        return hf, hb

    zero = jnp.zeros((1, LANES), F32)
    lax.fori_loop(0, nt, body, (zero, zero))

    def finish(j, c):
        rows = pl.ds(pl.multiple_of(j * rt, rt), rt)
        y = (hf_ref[rows, :] + hb_ref[rows, :]) * _gelu(ga_ref[rows, :])
        o_ref[rows, :] = y.astype(o_ref.dtype)
        return c

    lax.fori_loop(0, nt, finish, 0)


def _lru(z5, conv_w, conv_b, wgf, wgb, bgf, bgb, sp, rt):
    _, nb, seq, _ = z5.shape
    slab = lambda k: pl.BlockSpec((None, None, seq, LANES), lambda b, o: (k, b, 0, o))
    per_tile = lambda shape: pl.BlockSpec((None,) + shape, lambda b, o: (o,) + (0,) * len(shape))
    return pl.pallas_call(
        functools.partial(_lru_kernel, rt=rt),
        out_shape=jax.ShapeDtypeStruct((nb, seq, D_MODEL), BF16),
        grid=(nb, N_TILES),
        in_specs=[slab(0), slab(1),
                  pl.BlockSpec((4, LANES), lambda b, o: (0, o)),
                  pl.BlockSpec((1, LANES), lambda b, o: (0, o)),
                  per_tile((LANES, 2 * LANES)), per_tile((LANES, 2 * LANES)),
                  per_tile((1, 2 * LANES)), per_tile((1, 2 * LANES)),
                  pl.BlockSpec((2, LANES), lambda b, o: (0, o))],
        out_specs=pl.BlockSpec((None, seq, LANES), lambda b, o: (b, 0, o)),
        scratch_shapes=[pltpu.VMEM((seq, LANES), F32), pltpu.VMEM((seq, LANES), F32)],
        compiler_params=_params("arbitrary", "arbitrary"),
        name="lru",
    )(z5, z5, conv_w, conv_b, wgf, wgb, bgf, bgb, sp)


def _lru_weights(lru_wr, lru_br, lru_wi, lru_bi, lru_lam):
    eye = jnp.eye(2, dtype=F32)

    def blockdiag(w):
        w = w.reshape(N_TILES, 2, LRU_HEAD_DIM, LRU_HEAD_DIM)
        return jnp.einsum('ohij,hk->ohikj', w, eye).reshape(N_TILES, LANES, LANES)

    def direction(d):
        wg = jnp.concatenate([blockdiag(lru_wr[d]), blockdiag(lru_wi[d])], axis=-1).astype(BF16)
        bg = jnp.concatenate([lru_br[d].reshape(N_TILES, 1, LANES),
                              lru_bi[d].reshape(N_TILES, 1, LANES)], axis=-1)
        return wg, bg

    wgf, bgf = direction(0)
    wgb, bgb = direction(1)
    return wgf, wgb, bgf, bgb, jax.nn.softplus(-lru_lam)


def _cmul(ar, ai, br, bi):
    return ar * br - ai * bi, ar * bi + ai * br


def _s5_kernel(u_ref, wi_ref, wp_ref, wq_ref, tab_ref, o_ref, xcat_ref, st_ref, *, rb):
    seq = u_ref.shape[0]
    nc = seq // S5_CHUNK
    nblk = nc // rb
    ntile = nc // SUBLANES

    for p in range(S5_CHUNK):
        xcat_ref[:, p * LANES:(p + 1) * LANES] = u_ref[pl.ds(p, nc, stride=S5_CHUNK), :].astype(BF16)

    for k in range(nblk):
        rows = pl.ds(k * rb, rb)
        st_ref[rows, :] = jnp.dot(xcat_ref[rows, :], wp_ref[...], preferred_element_type=F32)

    row = lax.broadcasted_iota(jnp.int32, (SUBLANES, HALF_STATE), 0)

    def tile_scan(t, er, ei, d, reverse):
        rows = pl.ds(pl.multiple_of(t * SUBLANES, SUBLANES), SUBLANES)
        c0 = d * 2 * HALF_STATE
        xr = st_ref[rows, c0:c0 + HALF_STATE]
        xi = st_ref[rows, c0 + HALF_STATE:c0 + 2 * HALF_STATE]
        for m, s in enumerate((1, 2, 4)):
            shift = SUBLANES - s if reverse else s
            valid = (row < SUBLANES - s) if reverse else (row >= s)
            sr = jnp.where(valid, pltpu.roll(xr, shift, 0), 0.0)
            si = jnp.where(valid, pltpu.roll(xi, shift, 0), 0.0)
            pr = tab_ref[d, 0, m:m + 1, :]
            pi = tab_ref[d, 1, m:m + 1, :]
            mr, mi = _cmul(pr, pi, sr, si)
            xr = xr + mr
            xi = xi + mi
        shift = SUBLANES - 1 if reverse else 1
        valid = (row < SUBLANES - 1) if reverse else (row >= 1)
        qr = jnp.where(valid, pltpu.roll(xr, shift, 0), 0.0)
        qi = jnp.where(valid, pltpu.roll(xi, shift, 0), 0.0)
        wr = tab_ref[d, 0, 4:4 + SUBLANES, :]
        wi = tab_ref[d, 1, 4:4 + SUBLANES, :]
        cr, ci = _cmul(wr, wi, er, ei)
        st_ref[rows, c0:c0 + HALF_STATE] = qr + cr
        st_ref[rows, c0 + HALF_STATE:c0 + 2 * HALF_STATE] = qi + ci
        edge = 0 if reverse else SUBLANES - 1
        nr, ni = _cmul(tab_ref[d, 0, 3:4, :], tab_ref[d, 1, 3:4, :], er, ei)
        return nr + xr[edge:edge + 1], ni + xi[edge:edge + 1]

    def body(t, carry):
        fr, fi, br, bi = carry
        fr, fi = tile_scan(t, fr, fi, 0, False)
        br, bi = tile_scan(ntile - 1 - t, br, bi, 1, True)
        return fr, fi, br, bi

    zero = jnp.zeros((1, HALF_STATE), F32)
    lax.fori_loop(0, ntile, body, (zero, zero, zero, zero))

    for k in range(nblk):
        rows = pl.ds(k * rb, rb)
        y = jnp.dot(xcat_ref[rows, :], wi_ref[...], preferred_element_type=F32)
        y = y + jnp.dot(st_ref[rows, :].astype(BF16), wq_ref[...], preferred_element_type=F32)
        for p in range(S5_CHUNK):
            o_ref[pl.ds(k * rb * S5_CHUNK + p, rb, stride=S5_CHUNK), :] = y[:, p * LANES:(p + 1) * LANES]


def _s5(z5, wi, wp, wq, tab, rb):
    _, nb, seq, _ = z5.shape
    nc = seq // S5_CHUNK
    kdim = S5_CHUNK * LANES
    per_tile = lambda shape: pl.BlockSpec((None,) + shape, lambda o, b: (o,) + (0,) * len(shape))
    return pl.pallas_call(
        functools.partial(_s5_kernel, rb=rb),
        out_shape=jax.ShapeDtypeStruct((nb, seq, D_MODEL), F32),
        grid=(N_TILES, nb),
        in_specs=[pl.BlockSpec((None, None, seq, LANES), lambda o, b: (2, b, 0, o)),
                  per_tile((kdim, kdim)), per_tile((kdim, S5_STATE_COLS)),
                  per_tile((S5_STATE_COLS, kdim)), per_tile((2, 2, 16, HALF_STATE))],
        out_specs=pl.BlockSpec((None, seq, LANES), lambda o, b: (b, 0, o)),
        scratch_shapes=[pltpu.VMEM((nc, kdim), BF16), pltpu.VMEM((nc, S5_STATE_COLS), F32)],
        compiler_params=_params("arbitrary", "arbitrary"),
        name="s5",
    )(z5, wi, wp, wq, tab)


def _s5_weights(lam_re, lam_im, log_step, b_re, b_im, c_re, c_im):
    T = S5_CHUNK
    G, P, H = SSM_GROUPS, SSM_STATE, SSM_GROUP
    dt = jnp.exp(log_step)[:, :, None, None]
    kk = jnp.arange(8 * T + 1, dtype=F32)[None, None, :, None]
    mag = jnp.exp(lam_re[:, :, None, :] * dt * kk)
    ang = lam_im[:, :, None, :] * dt * kk
    pr = mag * jnp.cos(ang)
    pi = mag * jnp.sin(ang)
    ar, ai = pr[:, :, 1], pi[:, :, 1]
    den = lam_re * lam_re + lam_im * lam_im
    nr = ar - 1.0
    cr = (nr * lam_re + ai * lam_im) / den
    ci = (ai * lam_re - nr * lam_im) / den
    bbr = cr[..., None] * b_re - ci[..., None] * b_im
    bbi = cr[..., None] * b_im + ci[..., None] * b_re
    mr = pr[:, :, :T, :, None] * bbr[:, :, None] - pi[:, :, :T, :, None] * bbi[:, :, None]
    mi = pr[:, :, :T, :, None] * bbi[:, :, None] + pi[:, :, :T, :, None] * bbr[:, :, None]
    taps = jnp.einsum('gdon,gdknh->gdkoh', c_re, mr) - jnp.einsum('gdon,gdknh->gdkoh', c_im, mi)
    pos = jnp.arange(T)
    lag = pos[None, :] - pos[:, None]
    kf = taps[:, 0][:, jnp.clip(lag, 0, T - 1)] * (lag >= 0)[None, :, :, None, None]
    kb = taps[:, 1][:, jnp.clip(-lag, 0, T - 1)] * (lag <= 0)[None, :, :, None, None]
    intra = (kf + kb).transpose(0, 1, 4, 2, 3)
    eye = jnp.eye(GROUPS_PER_TILE, dtype=F32)
    intra = intra.reshape(N_TILES, GROUPS_PER_TILE, T, H, T, H)
    wi = jnp.einsum('ogphqk,gj->opghqjk', intra, eye).reshape(N_TILES, T * LANES, T * LANES)

    def pmat(m):
        f = m[:, 0, ::-1]
        b = m[:, 1]
        return jnp.stack([f, b], axis=1)
    pm = jnp.stack([pmat(mr), pmat(mi)], axis=2)
    pm = pm.transpose(0, 3, 5, 1, 2, 4).reshape(N_TILES, GROUPS_PER_TILE, T, H, 2, 2, P)
    wp = jnp.einsum('ogphdcn,gj->opghdcjn', pm, eye).reshape(N_TILES, T * LANES, S5_STATE_COLS)

    lag_f = pos + 1
    lag_b = T - pos
    def qpair(d, lags):
        prd = pr[:, d][:, lags]
        pid = pi[:, d][:, lags]
        on_re = c_re[:, d][:, None] * prd[:, :, None, :] - c_im[:, d][:, None] * pid[:, :, None, :]
        on_im = -(c_re[:, d][:, None] * pid[:, :, None, :] + c_im[:, d][:, None] * prd[:, :, None, :])
        return jnp.stack([on_re, on_im], axis=1)
    qm = jnp.stack([qpair(0, lag_f), qpair(1, lag_b)], axis=1)
    qm = qm.transpose(0, 1, 2, 5, 3, 4).reshape(N_TILES, GROUPS_PER_TILE, 2, 2, P, T, H)
    wq = jnp.einsum('ogdcnph,gj->odcgnpjh', qm, eye).reshape(N_TILES, S5_STATE_COLS, T * LANES)

    def lanes(x):
        x = x.reshape(N_TILES, GROUPS_PER_TILE, 2, x.shape[2], P)
        return x.transpose(0, 2, 3, 1, 4).reshape(N_TILES, 2, x.shape[3], HALF_STATE)
    steps = jnp.array([T, 2 * T, 4 * T, 8 * T])
    rows_f = T * jnp.arange(SUBLANES)
    rows_b = T * (SUBLANES - 1 - jnp.arange(SUBLANES))
    def table(p):
        head = p[:, :, steps]
        tail = jnp.stack([p[:, 0][:, rows_f], p[:, 1][:, rows_b]], axis=1)
        full = jnp.concatenate([head, tail, jnp.zeros_like(head)], axis=2)
        return lanes(full)
    tab = jnp.stack([table(pr), table(pi)], axis=2)
    return wi.astype(BF16), wp.astype(BF16), wq.astype(BF16), tab


def _merge_kernel(x_ref, ya_ref, ys_ref, ub_ref, g1_ref, g2_ref, wlo_ref, wga_ref, wgb_ref, wo_ref,
                  d_ref, gf_ref, x1_ref, h2t_ref):
    ya = jnp.dot(ya_ref[...], wlo_ref[...], preferred_element_type=F32)
    yg = _gelu(ys_ref[...] + d_ref[...] * ub_ref[...]).astype(BF16)
    yb = (jnp.dot(yg, wga_ref[...], preferred_element_type=F32)
          * jax.nn.sigmoid(jnp.dot(yg, wgb_ref[...], preferred_element_type=F32)))
    merged = jax.nn.sigmoid(g1_ref[...]) * ya + jax.nn.sigmoid(g2_ref[...]) * yb
    x1 = x_ref[...] + jnp.dot(merged.astype(BF16), wo_ref[...], preferred_element_type=F32)
    x1_ref[...] = x1
    h2t_ref[...] = _rms(x1, gf_ref[...]).T.astype(BF16)


def _merge(x2d, ya, ys, z5f, w_lru_out, w_glu_a, w_glu_b, w_out, s5_d, g_ffn, tb):
    n = x2d.shape[0]
    tok = pl.BlockSpec((tb, D_MODEL), lambda i: (i, 0))
    zsl = lambda k: pl.BlockSpec((None, tb, D_MODEL), lambda i: (k, i, 0))
    wsp = pl.BlockSpec((D_MODEL, D_MODEL), lambda i: (0, 0))
    vec = pl.BlockSpec((1, D_MODEL), lambda i: (0, 0))
    return pl.pallas_call(
        _merge_kernel,
        out_shape=(jax.ShapeDtypeStruct((n, D_MODEL), F32), jax.ShapeDtypeStruct((D_MODEL, n), BF16)),
        grid=(n // tb,),
        in_specs=[tok, tok, tok, zsl(2), zsl(3), zsl(4), wsp, wsp, wsp, wsp, vec, vec],
        out_specs=(tok, pl.BlockSpec((D_MODEL, tb), lambda i: (0, i))),
        compiler_params=_params("arbitrary"),
        name="merge",
    )(x2d, ya, ys, z5f, z5f, z5f, w_lru_out, w_glu_a, w_glu_b, w_out,
      s5_d.reshape(1, D_MODEL), g_ffn.reshape(1, D_MODEL))


def _top16(s):
    keys = lax.broadcasted_iota(jnp.int32, s.shape, 0)
    slot = lax.broadcasted_iota(jnp.int32, (PEER_TOPK, s.shape[1]), 0)

    def body(k, carry):
        s, rank, vals = carry
        m = jnp.max(s, axis=0, keepdims=True)
        first = jnp.min(jnp.where(s == m, keys, N_KEYS), axis=0, keepdims=True)
        sel = keys == first
        s = jnp.where(sel, -jnp.inf, s)
        rank = jnp.where(sel, k, rank)
        vals = jnp.where(slot == k, m, vals)
        return s, rank, vals

    init = (s, jnp.full(s.shape, PEER_TOPK, jnp.int32), jnp.zeros((PEER_TOPK, s.shape[1]), F32))
    _, rank, vals = lax.fori_loop(0, PEER_TOPK, body, init)
    return vals, rank


def _pair_select(v1, v2):
    w = v1.shape[1]
    r8 = lax.broadcasted_iota(jnp.int32, (SUBLANES, w), 0)
    neg = -jnp.inf
    tiles = []

    big = PEER_TOPK * PEER_TOPK

    def fixed_a(a, b0, limit):
        sums = v1[a:a + 1] + v2[b0:b0 + SUBLANES]
        b = r8 + b0
        ok = b < limit
        tiles.append((jnp.where(ok, sums, neg), jnp.where(ok, a * PEER_TOPK + b, big), a, None))

    def fixed_b(b, a0, lo, hi):
        sums = v1[a0:a0 + SUBLANES] + v2[b:b + 1]
        a = r8 + a0
        ok = jnp.where(a >= lo, a, hi) < hi
        tiles.append((jnp.where(ok, sums, neg), jnp.where(ok, a * PEER_TOPK + b, big), None, a0))

    fixed_a(0, 0, 16); fixed_a(0, 8, 16); fixed_a(1, 0, 8); fixed_a(2, 0, 5); fixed_a(3, 0, 4)
    fixed_b(0, 0, 4, 16); fixed_b(0, 8, 4, 16); fixed_b(1, 0, 4, 8); fixed_b(2, 0, 4, 5)

    top = v1[0:1] + v2[0:1]
    sums = [t[0] for t in tiles]
    picked = [jnp.zeros((SUBLANES, w), F32) for _ in tiles]
    for _ in range(PEER_TOPK):
        m = functools.reduce(jnp.maximum, sums)
        m = jnp.max(m, axis=0, keepdims=True)
        cand = [jnp.where(s == m, t[1], big) for s, t in zip(sums, tiles)]
        first = jnp.min(functools.reduce(jnp.minimum, cand), axis=0, keepdims=True)
        for n, t in enumerate(tiles):
            sel = t[1] == first
            picked[n] = jnp.where(sel, 1.0, picked[n])
            sums[n] = jnp.where(sel, neg, sums[n])

    r16 = lax.broadcasted_iota(jnp.int32, (PEER_TOPK, w), 0)
    cnt = jnp.zeros((PEER_TOPK, w), F32)
    den = jnp.zeros((SUBLANES, w), F32)
    lo_rows = jnp.zeros((SUBLANES, w), F32)
    hi_rows = jnp.zeros((SUBLANES, w), F32)
    for n, t in enumerate(tiles):
        den = den + picked[n] * jnp.exp(jnp.where(picked[n] > 0, t[0], top) - top)
        if t[2] is not None:
            cnt = cnt + jnp.where(r16 == t[2], jnp.sum(picked[n], axis=0, keepdims=True), 0.0)
        elif t[3] == 0:
            lo_rows = lo_rows + picked[n]
        else:
            hi_rows = hi_rows + picked[n]
    cnt = cnt + jnp.concatenate([lo_rows, hi_rows], axis=0)
    return cnt, jnp.sum(den, axis=0, keepdims=True)


def _sort_pairs(n):
    pairs = []

    def merge(lo, m, r):
        step = 2 * r
        if step < m:
            merge(lo, m, step)
            merge(lo + r, m, step)
            pairs.extend((i, i + r) for i in range(lo + r, lo + m - r, step))
        else:
            pairs.append((lo, lo + r))

    def sort(lo, m):
        if m > 1:
            sort(lo, m // 2)
            sort(lo + m // 2, m // 2)
            merge(lo, m, 1)

    sort(0, n)
    return pairs


_SORT16 = _sort_pairs(PEER_TOPK)


def _exchange(v, i, j):
    a, b = v[i], v[j]
    if b is None:
        return
    if a is None:
        v[i], v[j] = b, None
        return
    v[i], v[j] = jnp.maximum(a, b), jnp.minimum(a, b)


def _sorted_top16(tiles):
    v = list(tiles) + [None] * (PEER_TOPK - len(tiles))
    for i, j in _SORT16:
        _exchange(v, i, j)
    for s in (1, 2, 4):
        r = [None if x is None else pltpu.roll(x, s, 0) for x in v]
        merged = []
        for k in range(PEER_TOPK):
            a, b = v[k], r[PEER_TOPK - 1 - k]
            merged.append(b if a is None else a if b is None else jnp.maximum(a, b))
        v = merged
        for d in (8, 4, 2, 1):
            for i in range(PEER_TOPK):
                if not i & d:
                    _exchange(v, i, i + d)
    return v


def _dup_words(x):
    bits = pltpu.bitcast(x.astype(BF16).astype(F32), jnp.uint32)
    return bits | (bits >> 16)


def _rows_bf16(words, rows):
    return pltpu.bitcast(jnp.broadcast_to(words, (rows // 2, words.shape[1])), BF16)


def _route_fast(a1, a2):
    w = a1.shape[1]
    split = lambda a: [a[SUBLANES * k:SUBLANES * (k + 1)] for k in range(N_KEYS // SUBLANES)]
    t1 = _sorted_top16(split(a1))
    t2 = _sorted_top16(split(a2))
    row = lax.broadcasted_iota(jnp.int32, (SUBLANES, w), 0)

    def column(t, base):
        out = t[base]
        for r in range(1, SUBLANES):
            out = jnp.where(row == r, t[base + r], out)
        return out

    v1 = [column(t1, 0), column(t1, SUBLANES)]
    v2 = [column(t2, 0), column(t2, SUBLANES)]
    neg = -jnp.inf
    cands = [t1[0] + v2[0], t1[0] + v2[1], t1[1] + v2[0],
             jnp.where(row < 5, t1[2] + v2[0], neg), jnp.where(row < 4, t1[3] + v2[0], neg),
             jnp.where(row >= 4, v1[0] + t2[0], neg), v1[1] + t2[0],
             jnp.where(row >= 4, v1[0] + t2[1], neg), jnp.where(row == 4, v1[0] + t2[2], neg)]
    theta = _sorted_top16(cands)[PEER_TOPK - 1][0:1]
    top = t1[0] + t2[0]
    n_pairs = jnp.zeros((SUBLANES, w), F32)
    den = jnp.zeros((SUBLANES, w), F32)
    for c in cands:
        hit = c >= theta
        n_pairs = n_pairs + jnp.where(hit, 1.0, 0.0)
        den = den + jnp.where(hit, jnp.exp(jnp.where(hit, c, top) - top), 0.0)
    n_pairs = jnp.sum(n_pairs, axis=0, keepdims=True)
    inv_den = 1.0 / jnp.sum(den, axis=0, keepdims=True)

    in1 = a1 >= t1[PEER_TOPK - 1][0:1]
    in2 = a2 >= t2[PEER_TOPK - 1][0:1]
    n1 = jnp.sum(jnp.where(in1, 1.0, 0.0), axis=0, keepdims=True)
    n2 = jnp.sum(jnp.where(in2, 1.0, 0.0), axis=0, keepdims=True)
    ck = jnp.zeros(a1.shape, F32)
    r2 = jnp.zeros(a2.shape, F32)
    for b in range(PEER_TOPK):
        ck = ck + jnp.where(a1 + t2[b][0:1] >= theta, 1.0, 0.0)
        r2 = r2 + jnp.where(t2[b][0:1] > a2, 1.0, 0.0)
    ck = jnp.where(in1, ck, 0.0)
    pk = jnp.where(in1, jnp.exp(jnp.where(in1, a1, 0.0) - t1[0][0:1]), 0.0)
    qk = jnp.where(in2, jnp.exp(jnp.where(in2, a2, 0.0) - t2[0][0:1]) * inv_den, 0.0)
    k = float(PEER_TOPK)
    flag = jnp.where((n1 != k) | (n2 != k) | (n_pairs != k), 1.0, 0.0)
    return pk, ck, qk, r2, flag


def _route_exact(a1, a2):
    v1, rank1 = _top16(a1)
    v2, rank2 = _top16(a2)
    cnt, den = _pair_select(v1, v2)
    in1 = rank1 < PEER_TOPK
    in2 = rank2 < PEER_TOPK
    ck = jnp.zeros(a1.shape, F32)
    for a in range(PEER_TOPK):
        ck = jnp.where(rank1 == a, cnt[a:a + 1], ck)
    pk = jnp.where(in1, jnp.exp(jnp.where(in1, a1, v1[0:1]) - v1[0:1]), 0.0)
    qk = jnp.where(in2, jnp.exp(jnp.where(in2, a2, v2[0:1]) - v2[0:1]) / den, 0.0)
    return pk, ck, qk, rank2.astype(F32)


def _route_kernel(h2t_ref, wqt_ref, keys_ref, p_ref, c_ref, q_ref, r2_ref, s_ref):
    qt = jnp.dot(wqt_ref[...], h2t_ref[...], preferred_element_type=F32)
    s_ref[0] = jnp.dot(keys_ref[0], qt[:PEER_HALF].astype(BF16), preferred_element_type=F32)
    s_ref[1] = jnp.dot(keys_ref[1], qt[PEER_HALF:].astype(BF16), preferred_element_type=F32)
    ntile = s_ref.shape[2] // LANES

    def store(cols, pk, ck, qk, r2):
        p_ref[:, cols] = _dup_words(pk)
        c_ref[:, cols] = _dup_words(ck)
        q_ref[:, cols] = qk.astype(q_ref.dtype)
        r2_ref[:, cols] = r2.astype(r2_ref.dtype)

    flag = jnp.zeros((1, LANES), F32)
    for lt in range(ntile):
        cols = slice(lt * LANES, (lt + 1) * LANES)
        pk, ck, qk, r2, f = _route_fast(s_ref[0, :, cols], s_ref[1, :, cols])
        store(cols, pk, ck, qk, r2)
        flag = jnp.maximum(flag, f)

    @pl.when(jnp.max(flag) > 0.0)
    def _():
        for lt in range(ntile):
            cols = slice(lt * LANES, (lt + 1) * LANES)
            store(cols, *_route_exact(s_ref[0, :, cols], s_ref[1, :, cols]))


def _route(h2t, wqt, keys, tb):
    n = h2t.shape[1]
    words = jax.ShapeDtypeStruct((PEER_HEADS, N_KEYS, n), jnp.uint32)
    halfs = jax.ShapeDtypeStruct((PEER_HEADS, N_KEYS, n), BF16)
    osp = pl.BlockSpec((None, N_KEYS, tb), lambda i, h: (h, 0, i))
    return pl.pallas_call(
        _route_kernel,
        out_shape=(words, words, halfs, halfs),
        grid=(n // tb, PEER_HEADS),
        in_specs=[pl.BlockSpec((D_MODEL, tb), lambda i, h: (0, i)),
                  pl.BlockSpec((2 * PEER_HALF, D_MODEL), lambda i, h: (h, 0)),
                  pl.BlockSpec((None, 2, N_KEYS, PEER_HALF), lambda i, h: (h, 0, 0, 0))],
        out_specs=(osp, osp, osp, osp),
        scratch_shapes=[pltpu.VMEM((2, N_KEYS, tb), F32)],
        compiler_params=_params("arbitrary", "arbitrary"),
        name="route",
    )(h2t, wqt, keys)


def _peer_kernel(h2t_ref, u_ref, vt_ref, p_ref, c_ref, q_ref, r2_ref, x1_ref, g_ref, y_ref, acc_ref,
                 *, rows_per_step):
    ib = pl.program_id(1)

    @pl.when(ib == 0)
    def _():
        acc_ref[...] = jnp.zeros_like(acc_ref)

    act = jnp.dot(u_ref[...], h2t_ref[...], preferred_element_type=F32)
    parts = []
    for ii in range(rows_per_step):
        gate = None
        for h in range(PEER_HEADS):
            q = q_ref[h]
            hit = r2_ref[h] < _rows_bf16(c_ref[h, ii:ii + 1, :], N_KEYS)
            term = jnp.where(hit, q, jnp.zeros_like(q)) * _rows_bf16(p_ref[h, ii:ii + 1, :], N_KEYS)
            gate = term if gate is None else gate + term
        a = act[ii * N_KEYS:(ii + 1) * N_KEYS]
        parts.append(gate * _gelu(a).astype(BF16))
    wt = jnp.concatenate(parts, axis=0)
    acc_ref[...] += jnp.dot(vt_ref[...], wt, preferred_element_type=F32)

    @pl.when(ib == pl.num_programs(1) - 1)
    def _():
        x2 = x1_ref[...] + acc_ref[...].T
        y_ref[...] = _rms(x2, g_ref[...])


def _peer(h2t, u, vt, pk, ck, qk, r2k, x1, g_final, tb, rows_per_step):
    n = h2t.shape[1]
    eb = rows_per_step * N_KEYS
    small = pl.BlockSpec((PEER_HEADS, rows_per_step, tb), lambda i, e: (0, e, i))
    full = pl.BlockSpec((PEER_HEADS, N_KEYS, tb), lambda i, e: (0, 0, i))
    tok = pl.BlockSpec((tb, D_MODEL), lambda i, e: (i, 0))
    return pl.pallas_call(
        functools.partial(_peer_kernel, rows_per_step=rows_per_step),
        out_shape=jax.ShapeDtypeStruct((n, D_MODEL), F32),
        grid=(n // tb, N_KEYS // rows_per_step),
        in_specs=[pl.BlockSpec((D_MODEL, tb), lambda i, e: (0, i)),
                  pl.BlockSpec((eb, D_MODEL), lambda i, e: (e, 0)),
                  pl.BlockSpec((D_MODEL, eb), lambda i, e: (0, e)),
                  small, small, full, full, tok,
                  pl.BlockSpec((1, D_MODEL), lambda i, e: (0, 0))],
        out_specs=tok,
        scratch_shapes=[pltpu.VMEM((D_MODEL, tb), F32)],
        compiler_params=_params("arbitrary", "arbitrary"),
        name="peer",
    )(h2t, u, vt, pk, ck, qk, r2k, x1, g_final.reshape(1, D_MODEL))


def _layer(x, prm):
    nb, seq, _ = x.shape
    n = nb * seq
    x2d = x.reshape(n, D_MODEL)
    z5 = _in_proj(x2d, prm['g_mix'], prm['w_in'], 512)
    z5s = z5.reshape(z5.shape[0], nb, seq, D_MODEL)
    ya = _lru(z5s, prm['conv_w'], prm['conv_b'], *prm['lru'], rt=256)
    ys = _s5(z5s, *prm['s5'], rb=min(256, seq // S5_CHUNK))
    x1, h2t = _merge(x2d, ya.reshape(n, D_MODEL), ys.reshape(n, D_MODEL), z5,
                     prm['w_lru_out'], prm['w_glu_a'], prm['w_glu_b'], prm['w_out'],
                     prm['s5_d'], prm['g_ffn'], 256)
    pk, ck, qk, r2k = _route(h2t, prm['wqt'], prm['keys'], 512)
    y = _peer(h2t, prm['u'], prm['vt'], pk, ck, qk, r2k, x1, prm['g_final'], 512, 8)
    return y.reshape(nb, seq, D_MODEL)


def kernel(x_prompt, x_sample, g_mix, w_in, conv_w, conv_b, lru_wr, lru_br, lru_wi, lru_bi, lru_lam,
           w_lru_out, s5_lam_re, s5_lam_im, s5_log_step, s5_b_re, s5_b_im, s5_c_re, s5_c_im, s5_d,
           w_glu_a, w_glu_b, w_out, g_ffn, w_query, sub_keys, expert_u, expert_v, g_final):
    depth = g_mix.shape[0]
    xp, xs = x_prompt, x_sample
    for l in range(depth):
        prm = {
            'g_mix': g_mix[l], 'w_in': w_in[l].astype(BF16),
            'conv_w': conv_w[l], 'conv_b': conv_b[l].reshape(1, D_MODEL),
            'lru': _lru_weights(lru_wr[l], lru_br[l], lru_wi[l], lru_bi[l], lru_lam[l]),
            's5': _s5_weights(s5_lam_re[l], s5_lam_im[l], s5_log_step[l], s5_b_re[l], s5_b_im[l],
                              s5_c_re[l], s5_c_im[l]),
            'w_lru_out': w_lru_out[l].astype(BF16), 'w_glu_a': w_glu_a[l].astype(BF16),
            'w_glu_b': w_glu_b[l].astype(BF16), 'w_out': w_out[l].astype(BF16),
            's5_d': s5_d[l], 'g_ffn': g_ffn[l],
            'wqt': w_query[l].T.astype(BF16),
            'keys': sub_keys[l].astype(BF16),
            'u': expert_u[l].astype(BF16), 'vt': expert_v[l].T.astype(BF16),
            'g_final': g_final,
        }
        assert depth == 1
        xp = _layer(xp, prm)
        xs = _layer(xs, prm)
    return (xp, xs)
```

```python
import functools
import math

import jax
import jax.numpy as jnp
from jax import lax
from jax.experimental import pallas as pl
from jax.experimental.pallas import tpu as pltpu

F32 = jnp.float32
BF16 = jnp.bfloat16

LANES = 128
SUBLANES = 8
VMEM_LIMIT_BYTES = 56 * 1024 * 1024

D_MODEL = 1024
LRU_HEADS = 16
LRU_HEAD_DIM = D_MODEL // LRU_HEADS
LRU_C = 8.0
SSM_GROUP = 16
SSM_GROUPS = D_MODEL // SSM_GROUP
SSM_STATE = 64
N_KEYS = 128
PEER_HEADS = 8
PEER_TOPK = 16
PEER_HALF = 128
EPS = 1e-6

N_TILES = D_MODEL // LANES
GROUPS_PER_TILE = LANES // SSM_GROUP
S5_CHUNK = 8
S5_STATE_COLS = 2 * 2 * GROUPS_PER_TILE * SSM_STATE
HALF_STATE = GROUPS_PER_TILE * SSM_STATE


def _gelu(x):
    c = math.sqrt(2.0 / math.pi)
    return 0.5 * x * (1.0 + jnp.tanh(c * (x + 0.044715 * (x * x * x))))


def _rms(x, g):
    ms = jnp.mean(x * x, axis=-1, keepdims=True)
    return x * lax.rsqrt(ms + EPS) * g


def _params(*sem, flags=None):
    return pltpu.CompilerParams(dimension_semantics=sem, vmem_limit_bytes=VMEM_LIMIT_BYTES, flags=flags)


def _in_proj_kernel(x_ref, g_ref, w_ref, z_ref):
    h = _rms(x_ref[...], g_ref[...]).astype(BF16)
    for j in range(z_ref.shape[0]):
        z_ref[j] = jnp.dot(h, w_ref[:, j * D_MODEL:(j + 1) * D_MODEL], preferred_element_type=F32)


def _in_proj(x2d, g_mix, w_in, tb):
    n = x2d.shape[0]
    nj = w_in.shape[1] // D_MODEL
    return pl.pallas_call(
        _in_proj_kernel,
        out_shape=jax.ShapeDtypeStruct((nj, n, D_MODEL), F32),
        grid=(n // tb,),
        in_specs=[pl.BlockSpec((tb, D_MODEL), lambda i: (i, 0)),
                  pl.BlockSpec((1, D_MODEL), lambda i: (0, 0)),
                  pl.BlockSpec((D_MODEL, nj * D_MODEL), lambda i: (0, 0))],
        out_specs=pl.BlockSpec((nj, tb, D_MODEL), lambda i: (0, i, 0)),
        compiler_params=_params("arbitrary"),
        name="in_proj",
    )(x2d, g_mix.reshape(1, D_MODEL), w_in)


def _group_scan(a, b, reverse):
    row = lax.broadcasted_iota(jnp.int32, a.shape, 0) & (SUBLANES - 1)
    n = a.shape[0]
    for s in (1, 2, 4):
        if reverse:
            a_s = pltpu.roll(a, n - s, 0)
            b_s = pltpu.roll(b, n - s, 0)
            valid = row < SUBLANES - s
        else:
            a_s = pltpu.roll(a, s, 0)
            b_s = pltpu.roll(b, s, 0)
            valid = row >= s
        b = a * jnp.where(valid, b_s, 0.0) + b
        a = a * jnp.where(valid, a_s, 1.0)
    return a, b


def _carry_groups(a, b, h, reverse):
    groups = a.shape[0] // SUBLANES
    outs = [None] * groups
    order = range(groups - 1, -1, -1) if reverse else range(groups)
    edge = 0 if reverse else SUBLANES - 1
    for k in order:
        ak = a[k * SUBLANES:(k + 1) * SUBLANES]
        bk = b[k * SUBLANES:(k + 1) * SUBLANES]
        outs[k] = ak * h + bk
        h = ak[edge:edge + 1] * h + bk[edge:edge + 1]
    return jnp.concatenate(outs, axis=0), h


def _lru_kernel(xa_ref, ga_ref, cw_ref, cb_ref, wgf_ref, wgb_ref, bgf_ref, bgb_ref, sp_ref,
                o_ref, hf_ref, hb_ref, xc_ref, *, rt):
    seq = xa_ref.shape[0]
    nt = seq // rt
    cw = cw_ref[...]
    cb = cb_ref[...]

    def conv_tile(j, c):
        r0 = pl.multiple_of(j * rt, rt)
        cur = xa_ref[pl.ds(r0, rt), :]
        p0 = pl.multiple_of(jnp.maximum(r0 - SUBLANES, 0), SUBLANES)
        n0 = pl.multiple_of(jnp.minimum(r0 + rt, seq - SUBLANES), SUBLANES)
        prev = jnp.where(j > 0, xa_ref[pl.ds(p0, SUBLANES), :], 0.0)
        nxt = jnp.where(j < nt - 1, xa_ref[pl.ds(n0, SUBLANES), :], 0.0)
        ext = jnp.concatenate([prev, cur, nxt], axis=0)
        n = rt + 2 * SUBLANES
        xm2 = pltpu.roll(ext, 2, 0)[SUBLANES:SUBLANES + rt]
        xm1 = pltpu.roll(ext, 1, 0)[SUBLANES:SUBLANES + rt]
        xp1 = pltpu.roll(ext, n - 1, 0)[SUBLANES:SUBLANES + rt]
        xc_ref[pl.ds(r0, rt), :] = cw[0:1] * xm2 + cw[1:2] * xm1 + cw[2:3] * cur + cw[3:4] * xp1 + cb
        return c

    lax.fori_loop(0, nt, conv_tile, 0)

    def sigmoid(x):
        return 0.5 * jnp.tanh(0.5 * x) + 0.5

    def direction(j, h, wg_ref, bg_ref, sp, dst_ref, reverse):
        rows = pl.ds(pl.multiple_of(j * rt, rt), rt)
        xc = xc_ref[rows, :]
        gz = jnp.dot(xc.astype(BF16), wg_ref[...], preferred_element_type=F32) + bg_ref[...]
        r = sigmoid(gz[:, :LANES])
        i = sigmoid(gz[:, LANES:])
        log_a = -LRU_C * r * sp
        a = jnp.exp(log_a)
        v = -jnp.tanh(log_a) * (1.0 + a * a)
        b = jnp.where(v > 0.0, v * lax.rsqrt(v), 0.0) * i * xc
        a_g, b_g = _group_scan(a, b, reverse)
        h_tile, h = _carry_groups(a_g, b_g, h, reverse)
        dst_ref[rows, :] = h_tile
        return h

    def body(j, carry):
        hf, hb = carry
        hf = direction(j, hf, wgf_ref, bgf_ref, sp_ref[0:1, :], hf_ref, False)
        hb = direction(nt - 1 - j, hb, wgb_ref, bgb_ref, sp_ref[1:2, :], hb_ref, True)
        return hf, hb

    zero = jnp.zeros((1, LANES), F32)
    lax.fori_loop(0, nt, body, (zero, zero), unroll=2)

    def finish(j, c):
        rows = pl.ds(pl.multiple_of(j * rt, rt), rt)
        y = (hf_ref[rows, :] + hb_ref[rows, :]) * _gelu(ga_ref[rows, :])
        o_ref[rows, :] = y.astype(o_ref.dtype)
        return c

    lax.fori_loop(0, nt, finish, 0)


def _lru(z5, conv_w, conv_b, wgf, wgb, bgf, bgb, sp, rt):
    _, nb, seq, _ = z5.shape
    slab = lambda k: pl.BlockSpec((None, None, seq, LANES), lambda b, o: (k, b, 0, o))
    per_tile = lambda shape: pl.BlockSpec((None,) + shape, lambda b, o: (o,) + (0,) * len(shape))
    return pl.pallas_call(
        functools.partial(_lru_kernel, rt=rt),
        out_shape=jax.ShapeDtypeStruct((nb, seq, D_MODEL), BF16),
        grid=(nb, N_TILES),
        in_specs=[slab(0), slab(1),
                  pl.BlockSpec((4, LANES), lambda b, o: (0, o)),
                  pl.BlockSpec((1, LANES), lambda b, o: (0, o)),
                  per_tile((LANES, 2 * LANES)), per_tile((LANES, 2 * LANES)),
                  per_tile((1, 2 * LANES)), per_tile((1, 2 * LANES)),
                  pl.BlockSpec((2, LANES), lambda b, o: (0, o))],
        out_specs=pl.BlockSpec((None, seq, LANES), lambda b, o: (b, 0, o)),
        scratch_shapes=[pltpu.VMEM((seq, LANES), F32)] * 3,
        compiler_params=_params("arbitrary", "arbitrary"),
        name="lru",
    )(z5, z5, conv_w, conv_b, wgf, wgb, bgf, bgb, sp)


def _lru_weights(lru_wr, lru_br, lru_wi, lru_bi, lru_lam):
    eye = jnp.eye(2, dtype=F32)

    def blockdiag(w):
        w = w.reshape(N_TILES, 2, LRU_HEAD_DIM, LRU_HEAD_DIM)
        return jnp.einsum('ohij,hk->ohikj', w, eye).reshape(N_TILES, LANES, LANES)

    def direction(d):
        wg = jnp.concatenate([blockdiag(lru_wr[d]), blockdiag(lru_wi[d])], axis=-1).astype(BF16)
        bg = jnp.concatenate([lru_br[d].reshape(N_TILES, 1, LANES),
                              lru_bi[d].reshape(N_TILES, 1, LANES)], axis=-1)
        return wg, bg

    wgf, bgf = direction(0)
    wgb, bgb = direction(1)
    return wgf, wgb, bgf, bgb, jax.nn.softplus(-lru_lam)


def _cmul(ar, ai, br, bi):
    return ar * br - ai * bi, ar * bi + ai * br


def _s5_kernel(u_ref, wi_ref, wp_ref, wq_ref, tab_ref, o_ref, xcat_ref, st_ref, *, rb):
    seq = u_ref.shape[0]
    nc = seq // S5_CHUNK
    nblk = nc // rb
    ntile = nc // SUBLANES

    for p in range(S5_CHUNK):
        xcat_ref[:, p * LANES:(p + 1) * LANES] = u_ref[pl.ds(p, nc, stride=S5_CHUNK), :].astype(BF16)

    for k in range(nblk):
        rows = pl.ds(k * rb, rb)
        st_ref[rows, :] = jnp.dot(xcat_ref[rows, :], wp_ref[...], preferred_element_type=F32)

    row = lax.broadcasted_iota(jnp.int32, (SUBLANES, HALF_STATE), 0)

    def tile_scan(t, er, ei, d, reverse):
        rows = pl.ds(pl.multiple_of(t * SUBLANES, SUBLANES), SUBLANES)
        c0 = d * 2 * HALF_STATE
        xr = st_ref[rows, c0:c0 + HALF_STATE]
        xi = st_ref[rows, c0 + HALF_STATE:c0 + 2 * HALF_STATE]
        for m, s in enumerate((1, 2, 4)):
            shift = SUBLANES - s if reverse else s
            valid = (row < SUBLANES - s) if reverse else (row >= s)
            sr = jnp.where(valid, pltpu.roll(xr, shift, 0), 0.0)
            si = jnp.where(valid, pltpu.roll(xi, shift, 0), 0.0)
            pr = tab_ref[d, 0, m:m + 1, :]
            pi = tab_ref[d, 1, m:m + 1, :]
            mr, mi = _cmul(pr, pi, sr, si)
            xr = xr + mr
            xi = xi + mi
        shift = SUBLANES - 1 if reverse else 1
        valid = (row < SUBLANES - 1) if reverse else (row >= 1)
        qr = jnp.where(valid, pltpu.roll(xr, shift, 0), 0.0)
        qi = jnp.where(valid, pltpu.roll(xi, shift, 0), 0.0)
        wr = tab_ref[d, 0, 4:4 + SUBLANES, :]
        wi = tab_ref[d, 1, 4:4 + SUBLANES, :]
        cr, ci = _cmul(wr, wi, er, ei)
        st_ref[rows, c0:c0 + HALF_STATE] = qr + cr
        st_ref[rows, c0 + HALF_STATE:c0 + 2 * HALF_STATE] = qi + ci
        edge = 0 if reverse else SUBLANES - 1
        nr, ni = _cmul(tab_ref[d, 0, 3:4, :], tab_ref[d, 1, 3:4, :], er, ei)
        return nr + xr[edge:edge + 1], ni + xi[edge:edge + 1]

    def body(t, carry):
        fr, fi, br, bi = carry
        fr, fi = tile_scan(t, fr, fi, 0, False)
        br, bi = tile_scan(ntile - 1 - t, br, bi, 1, True)
        return fr, fi, br, bi

    zero = jnp.zeros((1, HALF_STATE), F32)
    lax.fori_loop(0, ntile, body, (zero, zero, zero, zero))

    for k in range(nblk):
        rows = pl.ds(k * rb, rb)
        y = jnp.dot(xcat_ref[rows, :], wi_ref[...], preferred_element_type=F32)
        y = y + jnp.dot(st_ref[rows, :].astype(BF16), wq_ref[...], preferred_element_type=F32)
        for p in range(S5_CHUNK):
            o_ref[pl.ds(k * rb * S5_CHUNK + p, rb, stride=S5_CHUNK), :] = y[:, p * LANES:(p + 1) * LANES]


def _s5(z5, wi, wp, wq, tab, rb):
    _, nb, seq, _ = z5.shape
    nc = seq // S5_CHUNK
    kdim = S5_CHUNK * LANES
    per_tile = lambda shape: pl.BlockSpec((None,) + shape, lambda o, b: (o,) + (0,) * len(shape))
    return pl.pallas_call(
        functools.partial(_s5_kernel, rb=rb),
        out_shape=jax.ShapeDtypeStruct((nb, seq, D_MODEL), F32),
        grid=(N_TILES, nb),
        in_specs=[pl.BlockSpec((None, None, seq, LANES), lambda o, b: (2, b, 0, o)),
                  per_tile((kdim, kdim)), per_tile((kdim, S5_STATE_COLS)),
                  per_tile((S5_STATE_COLS, kdim)), per_tile((2, 2, 16, HALF_STATE))],
        out_specs=pl.BlockSpec((None, seq, LANES), lambda o, b: (b, 0, o)),
        scratch_shapes=[pltpu.VMEM((nc, kdim), BF16), pltpu.VMEM((nc, S5_STATE_COLS), F32)],
        compiler_params=_params("arbitrary", "arbitrary"),
        name="s5",
    )(z5, wi, wp, wq, tab)


def _s5_weights(lam_re, lam_im, log_step, b_re, b_im, c_re, c_im):
    T = S5_CHUNK
    G, P, H = SSM_GROUPS, SSM_STATE, SSM_GROUP
    dt = jnp.exp(log_step)[:, :, None, None]
    kk = jnp.arange(8 * T + 1, dtype=F32)[None, None, :, None]
    mag = jnp.exp(lam_re[:, :, None, :] * dt * kk)
    ang = lam_im[:, :, None, :] * dt * kk
    pr = mag * jnp.cos(ang)
    pi = mag * jnp.sin(ang)
    ar, ai = pr[:, :, 1], pi[:, :, 1]
    den = lam_re * lam_re + lam_im * lam_im
    nr = ar - 1.0
    cr = (nr * lam_re + ai * lam_im) / den
    ci = (ai * lam_re - nr * lam_im) / den
    bbr = cr[..., None] * b_re - ci[..., None] * b_im
    bbi = cr[..., None] * b_im + ci[..., None] * b_re
    mr = pr[:, :, :T, :, None] * bbr[:, :, None] - pi[:, :, :T, :, None] * bbi[:, :, None]
    mi = pr[:, :, :T, :, None] * bbi[:, :, None] + pi[:, :, :T, :, None] * bbr[:, :, None]
    taps = jnp.einsum('gdon,gdknh->gdkoh', c_re, mr) - jnp.einsum('gdon,gdknh->gdkoh', c_im, mi)
    pos = jnp.arange(T)
    lag = pos[None, :] - pos[:, None]
    kf = taps[:, 0][:, jnp.clip(lag, 0, T - 1)] * (lag >= 0)[None, :, :, None, None]
    kb = taps[:, 1][:, jnp.clip(-lag, 0, T - 1)] * (lag <= 0)[None, :, :, None, None]
    intra = (kf + kb).transpose(0, 1, 4, 2, 3)
    eye = jnp.eye(GROUPS_PER_TILE, dtype=F32)
    intra = intra.reshape(N_TILES, GROUPS_PER_TILE, T, H, T, H)
    wi = jnp.einsum('ogphqk,gj->opghqjk', intra, eye).reshape(N_TILES, T * LANES, T * LANES)

    def pmat(m):
        f = m[:, 0, ::-1]
        b = m[:, 1]
        return jnp.stack([f, b], axis=1)
    pm = jnp.stack([pmat(mr), pmat(mi)], axis=2)
    pm = pm.transpose(0, 3, 5, 1, 2, 4).reshape(N_TILES, GROUPS_PER_TILE, T, H, 2, 2, P)
    wp = jnp.einsum('ogphdcn,gj->opghdcjn', pm, eye).reshape(N_TILES, T * LANES, S5_STATE_COLS)

    lag_f = pos + 1
    lag_b = T - pos
    def qpair(d, lags):
        prd = pr[:, d][:, lags]
        pid = pi[:, d][:, lags]
        on_re = c_re[:, d][:, None] * prd[:, :, None, :] - c_im[:, d][:, None] * pid[:, :, None, :]
        on_im = -(c_re[:, d][:, None] * pid[:, :, None, :] + c_im[:, d][:, None] * prd[:, :, None, :])
        return jnp.stack([on_re, on_im], axis=1)
    qm = jnp.stack([qpair(0, lag_f), qpair(1, lag_b)], axis=1)
    qm = qm.transpose(0, 1, 2, 5, 3, 4).reshape(N_TILES, GROUPS_PER_TILE, 2, 2, P, T, H)
    wq = jnp.einsum('ogdcnph,gj->odcgnpjh', qm, eye).reshape(N_TILES, S5_STATE_COLS, T * LANES)

    def lanes(x):
        x = x.reshape(N_TILES, GROUPS_PER_TILE, 2, x.shape[2], P)
        return x.transpose(0, 2, 3, 1, 4).reshape(N_TILES, 2, x.shape[3], HALF_STATE)
    steps = jnp.array([T, 2 * T, 4 * T, 8 * T])
    rows_f = T * jnp.arange(SUBLANES)
    rows_b = T * (SUBLANES - 1 - jnp.arange(SUBLANES))
    def table(p):
        head = p[:, :, steps]
        tail = jnp.stack([p[:, 0][:, rows_f], p[:, 1][:, rows_b]], axis=1)
        full = jnp.concatenate([head, tail, jnp.zeros_like(head)], axis=2)
        return lanes(full)
    tab = jnp.stack([table(pr), table(pi)], axis=2)
    return wi.astype(BF16), wp.astype(BF16), wq.astype(BF16), tab


def _merge_kernel(x_ref, ya_ref, ys_ref, ub_ref, g1_ref, g2_ref, wlo_ref, wga_ref, wgb_ref, wo_ref,
                  d_ref, gf_ref, x1_ref, h2t_ref):
    ya = jnp.dot(ya_ref[...], wlo_ref[...], preferred_element_type=F32)
    yg = _gelu(ys_ref[...] + d_ref[...] * ub_ref[...]).astype(BF16)
    yb = (jnp.dot(yg, wga_ref[...], preferred_element_type=F32)
          * jax.nn.sigmoid(jnp.dot(yg, wgb_ref[...], preferred_element_type=F32)))
    merged = jax.nn.sigmoid(g1_ref[...]) * ya + jax.nn.sigmoid(g2_ref[...]) * yb
    x1 = x_ref[...] + jnp.dot(merged.astype(BF16), wo_ref[...], preferred_element_type=F32)
    x1_ref[...] = x1
    h2t_ref[...] = _rms(x1, gf_ref[...]).T.astype(BF16)


def _merge(x2d, ya, ys, z5f, w_lru_out, w_glu_a, w_glu_b, w_out, s5_d, g_ffn, tb):
    n = x2d.shape[0]
    tok = pl.BlockSpec((tb, D_MODEL), lambda i: (i, 0))
    zsl = lambda k: pl.BlockSpec((None, tb, D_MODEL), lambda i: (k, i, 0))
    wsp = pl.BlockSpec((D_MODEL, D_MODEL), lambda i: (0, 0))
    vec = pl.BlockSpec((1, D_MODEL), lambda i: (0, 0))
    return pl.pallas_call(
        _merge_kernel,
        out_shape=(jax.ShapeDtypeStruct((n, D_MODEL), F32), jax.ShapeDtypeStruct((D_MODEL, n), BF16)),
        grid=(n // tb,),
        in_specs=[tok, tok, tok, zsl(2), zsl(3), zsl(4), wsp, wsp, wsp, wsp, vec, vec],
        out_specs=(tok, pl.BlockSpec((D_MODEL, tb), lambda i: (0, i))),
        compiler_params=_params("arbitrary"),
        name="merge",
    )(x2d, ya, ys, z5f, z5f, z5f, w_lru_out, w_glu_a, w_glu_b, w_out,
      s5_d.reshape(1, D_MODEL), g_ffn.reshape(1, D_MODEL))


def _top16(s):
    keys = lax.broadcasted_iota(jnp.int32, s.shape, 0)
    slot = lax.broadcasted_iota(jnp.int32, (PEER_TOPK, s.shape[1]), 0)

    def body(k, carry):
        s, rank, vals = carry
        m = jnp.max(s, axis=0, keepdims=True)
        first = jnp.min(jnp.where(s == m, keys, N_KEYS), axis=0, keepdims=True)
        sel = keys == first
        s = jnp.where(sel, -jnp.inf, s)
        rank = jnp.where(sel, k, rank)
        vals = jnp.where(slot == k, m, vals)
        return s, rank, vals

    init = (s, jnp.full(s.shape, PEER_TOPK, jnp.int32), jnp.zeros((PEER_TOPK, s.shape[1]), F32))
    _, rank, vals = lax.fori_loop(0, PEER_TOPK, body, init)
    return vals, rank


def _pair_select(v1, v2):
    w = v1.shape[1]
    r8 = lax.broadcasted_iota(jnp.int32, (SUBLANES, w), 0)
    neg = -jnp.inf
    tiles = []

    big = PEER_TOPK * PEER_TOPK

    def fixed_a(a, b0, limit):
        sums = v1[a:a + 1] + v2[b0:b0 + SUBLANES]
        b = r8 + b0
        ok = b < limit
        tiles.append((jnp.where(ok, sums, neg), jnp.where(ok, a * PEER_TOPK + b, big), a, None))

    def fixed_b(b, a0, lo, hi):
        sums = v1[a0:a0 + SUBLANES] + v2[b:b + 1]
        a = r8 + a0
        ok = jnp.where(a >= lo, a, hi) < hi
        tiles.append((jnp.where(ok, sums, neg), jnp.where(ok, a * PEER_TOPK + b, big), None, a0))

    fixed_a(0, 0, 16); fixed_a(0, 8, 16); fixed_a(1, 0, 8); fixed_a(2, 0, 5); fixed_a(3, 0, 4)
    fixed_b(0, 0, 4, 16); fixed_b(0, 8, 4, 16); fixed_b(1, 0, 4, 8); fixed_b(2, 0, 4, 5)

    top = v1[0:1] + v2[0:1]
    sums = [t[0] for t in tiles]
    picked = [jnp.zeros((SUBLANES, w), F32) for _ in tiles]
    for _ in range(PEER_TOPK):
        m = functools.reduce(jnp.maximum, sums)
        m = jnp.max(m, axis=0, keepdims=True)
        cand = [jnp.where(s == m, t[1], big) for s, t in zip(sums, tiles)]
        first = jnp.min(functools.reduce(jnp.minimum, cand), axis=0, keepdims=True)
        for n, t in enumerate(tiles):
            sel = t[1] == first
            picked[n] = jnp.where(sel, 1.0, picked[n])
            sums[n] = jnp.where(sel, neg, sums[n])

    r16 = lax.broadcasted_iota(jnp.int32, (PEER_TOPK, w), 0)
    cnt = jnp.zeros((PEER_TOPK, w), F32)
    den = jnp.zeros((SUBLANES, w), F32)
    lo_rows = jnp.zeros((SUBLANES, w), F32)
    hi_rows = jnp.zeros((SUBLANES, w), F32)
    for n, t in enumerate(tiles):
        den = den + picked[n] * jnp.exp(jnp.where(picked[n] > 0, t[0], top) - top)
        if t[2] is not None:
            cnt = cnt + jnp.where(r16 == t[2], jnp.sum(picked[n], axis=0, keepdims=True), 0.0)
        elif t[3] == 0:
            lo_rows = lo_rows + picked[n]
        else:
            hi_rows = hi_rows + picked[n]
    cnt = cnt + jnp.concatenate([lo_rows, hi_rows], axis=0)
    return cnt, jnp.sum(den, axis=0, keepdims=True)


def _sort_pairs(n):
    pairs = []

    def merge(lo, m, r):
        step = 2 * r
        if step < m:
            merge(lo, m, step)
            merge(lo + r, m, step)
            pairs.extend((i, i + r) for i in range(lo + r, lo + m - r, step))
        else:
            pairs.append((lo, lo + r))

    def sort(lo, m):
        if m > 1:
            sort(lo, m // 2)
            sort(lo + m // 2, m // 2)
            merge(lo, m, 1)

    sort(0, n)
    return pairs


_SORT16 = _sort_pairs(PEER_TOPK)


def _exchange(v, i, j):
    a, b = v[i], v[j]
    if b is None:
        return
    if a is None:
        v[i], v[j] = b, None
        return
    v[i], v[j] = jnp.maximum(a, b), jnp.minimum(a, b)


def _sorted_top16(tiles):
    v = list(tiles) + [None] * (PEER_TOPK - len(tiles))
    for i, j in _SORT16:
        _exchange(v, i, j)
    for s in (1, 2, 4):
        r = [None if x is None else pltpu.roll(x, s, 0) for x in v]
        merged = []
        for k in range(PEER_TOPK):
            a, b = v[k], r[PEER_TOPK - 1 - k]
            merged.append(b if a is None else a if b is None else jnp.maximum(a, b))
        v = merged
        for d in (8, 4, 2, 1):
            for i in range(PEER_TOPK):
                if not i & d:
                    _exchange(v, i, i + d)
    return v


def _dup_words(x):
    bits = pltpu.bitcast(x.astype(BF16).astype(F32), jnp.uint32)
    return bits | (bits >> 16)


def _rows_bf16(words, rows):
    return pltpu.bitcast(jnp.broadcast_to(words, (rows // 2, words.shape[1])), BF16)


def _route_fast(a1, a2):
    w = a1.shape[1]
    split = lambda a: [a[SUBLANES * k:SUBLANES * (k + 1)] for k in range(N_KEYS // SUBLANES)]
    t1 = _sorted_top16(split(a1))
    t2 = _sorted_top16(split(a2))
    row = lax.broadcasted_iota(jnp.int32, (SUBLANES, w), 0)

    def column(t, base):
        out = t[base]
        for r in range(1, SUBLANES):
            out = jnp.where(row == r, t[base + r], out)
        return out

    v1 = [column(t1, 0), column(t1, SUBLANES)]
    v2 = [column(t2, 0), column(t2, SUBLANES)]
    neg = -jnp.inf
    cands = [t1[0] + v2[0], t1[0] + v2[1], t1[1] + v2[0],
             jnp.where(row < 5, t1[2] + v2[0], neg), jnp.where(row < 4, t1[3] + v2[0], neg),
             jnp.where(row >= 4, v1[0] + t2[0], neg), v1[1] + t2[0],
             jnp.where(row >= 4, v1[0] + t2[1], neg), jnp.where(row == 4, v1[0] + t2[2], neg)]
    theta = _sorted_top16(cands)[PEER_TOPK - 1][0:1]
    top = t1[0] + t2[0]
    n_pairs = jnp.zeros((SUBLANES, w), F32)
    den = jnp.zeros((SUBLANES, w), F32)
    for c in cands:
        hit = c >= theta
        n_pairs = n_pairs + jnp.where(hit, 1.0, 0.0)
        den = den + jnp.where(hit, jnp.exp(jnp.where(hit, c, top) - top), 0.0)
    n_pairs = jnp.sum(n_pairs, axis=0, keepdims=True)
    inv_den = 1.0 / jnp.sum(den, axis=0, keepdims=True)

    in1 = a1 >= t1[PEER_TOPK - 1][0:1]
    in2 = a2 >= t2[PEER_TOPK - 1][0:1]
    n1 = jnp.sum(jnp.where(in1, 1.0, 0.0), axis=0, keepdims=True)
    n2 = jnp.sum(jnp.where(in2, 1.0, 0.0), axis=0, keepdims=True)
    ck = jnp.zeros(a1.shape, F32)
    r2 = jnp.zeros(a2.shape, F32)
    for b in range(PEER_TOPK):
        ck = ck + jnp.where(a1 + t2[b][0:1] >= theta, 1.0, 0.0)
        r2 = r2 + jnp.where(t2[b][0:1] > a2, 1.0, 0.0)
    ck = jnp.where(in1, ck, 0.0)
    pk = jnp.where(in1, jnp.exp(jnp.where(in1, a1, 0.0) - t1[0][0:1]), 0.0)
    qk = jnp.where(in2, jnp.exp(jnp.where(in2, a2, 0.0) - t2[0][0:1]) * inv_den, 0.0)
    k = float(PEER_TOPK)
    flag = jnp.where((n1 != k) | (n2 != k) | (n_pairs != k), 1.0, 0.0)
    return pk, ck, qk, r2, flag


def _route_exact(a1, a2):
    v1, rank1 = _top16(a1)
    v2, rank2 = _top16(a2)
    cnt, den = _pair_select(v1, v2)
    in1 = rank1 < PEER_TOPK
    in2 = rank2 < PEER_TOPK
    ck = jnp.zeros(a1.shape, F32)
    for a in range(PEER_TOPK):
        ck = jnp.where(rank1 == a, cnt[a:a + 1], ck)
    pk = jnp.where(in1, jnp.exp(jnp.where(in1, a1, v1[0:1]) - v1[0:1]), 0.0)
    qk = jnp.where(in2, jnp.exp(jnp.where(in2, a2, v2[0:1]) - v2[0:1]) / den, 0.0)
    return pk, ck, qk, rank2.astype(F32)


def _route_kernel(h2t_ref, wqt_ref, keys_ref, p_ref, c_ref, q_ref, r2_ref, s_ref):
    qt = jnp.dot(wqt_ref[...], h2t_ref[...], preferred_element_type=F32)
    s_ref[0] = jnp.dot(keys_ref[0], qt[:PEER_HALF].astype(BF16), preferred_element_type=F32)
    s_ref[1] = jnp.dot(keys_ref[1], qt[PEER_HALF:].astype(BF16), preferred_element_type=F32)
    ntile = s_ref.shape[2] // LANES

    def store(cols, pk, ck, qk, r2):
        p_ref[:, cols] = _dup_words(pk)
        c_ref[:, cols] = _dup_words(ck)
        q_ref[:, cols] = qk.astype(q_ref.dtype)
        r2_ref[:, cols] = r2.astype(r2_ref.dtype)

    flag = jnp.zeros((1, LANES), F32)
    for lt in range(ntile):
        cols = slice(lt * LANES, (lt + 1) * LANES)
        pk, ck, qk, r2, f = _route_fast(s_ref[0, :, cols], s_ref[1, :, cols])
        store(cols, pk, ck, qk, r2)
        flag = jnp.maximum(flag, f)

    @pl.when(jnp.max(flag) > 0.0)
    def _():
        for lt in range(ntile):
            cols = slice(lt * LANES, (lt + 1) * LANES)
            store(cols, *_route_exact(s_ref[0, :, cols], s_ref[1, :, cols]))


def _route(h2t, wqt, keys, tb):
    n = h2t.shape[1]
    words = jax.ShapeDtypeStruct((PEER_HEADS, N_KEYS, n), jnp.uint32)
    halfs = jax.ShapeDtypeStruct((PEER_HEADS, N_KEYS, n), BF16)
    osp = pl.BlockSpec((None, N_KEYS, tb), lambda i, h: (h, 0, i))
    return pl.pallas_call(
        _route_kernel,
        out_shape=(words, words, halfs, halfs),
        grid=(n // tb, PEER_HEADS),
        in_specs=[pl.BlockSpec((D_MODEL, tb), lambda i, h: (0, i)),
                  pl.BlockSpec((2 * PEER_HALF, D_MODEL), lambda i, h: (h, 0)),
                  pl.BlockSpec((None, 2, N_KEYS, PEER_HALF), lambda i, h: (h, 0, 0, 0))],
        out_specs=(osp, osp, osp, osp),
        scratch_shapes=[pltpu.VMEM((2, N_KEYS, tb), F32)],
        compiler_params=_params("arbitrary", "arbitrary"),
        name="route",
    )(h2t, wqt, keys)


def _peer_kernel(h2t_ref, u_ref, vt_ref, p_ref, c_ref, q_ref, r2_ref, x1_ref, g_ref, y_ref, acc_ref,
                 act_a, act_b, *, rows_per_step):
    e = pl.program_id(1)
    last = pl.num_programs(1) - 1

    def score(dst_ref):
        dst_ref[...] = jnp.dot(u_ref[...], h2t_ref[...], preferred_element_type=F32)

    def combine(src_ref):
        parts = []
        for ii in range(rows_per_step):
            gate = None
            for h in range(PEER_HEADS):
                q = q_ref[h]
                hit = r2_ref[h] < _rows_bf16(c_ref[h, ii:ii + 1, :], N_KEYS)
                term = jnp.where(hit, q, jnp.zeros_like(q)) * _rows_bf16(p_ref[h, ii:ii + 1, :], N_KEYS)
                gate = term if gate is None else gate + term
            a = src_ref[ii * N_KEYS:(ii + 1) * N_KEYS, :].astype(BF16)
            parts.append(gate * _gelu(a))
        wt = jnp.concatenate(parts, axis=0)
        acc_ref[...] += jnp.dot(vt_ref[...], wt, preferred_element_type=F32)

    @pl.when(e == 0)
    def _():
        acc_ref[...] = jnp.zeros_like(acc_ref)
        score(act_a)

    @pl.when((e % 2 == 1) & (e < last))
    def _():
        score(act_b)
        combine(act_a)

    @pl.when((e % 2 == 0) & (e > 0) & (e < last))
    def _():
        score(act_a)
        combine(act_b)

    @pl.when(e == last)
    def _():
        combine(act_b if (N_KEYS // rows_per_step) % 2 == 0 else act_a)
        x2 = x1_ref[...] + acc_ref[...].T
        y_ref[...] = _rms(x2, g_ref[...])


def _peer(h2t, u, vt, pk, ck, qk, r2k, x1, g_final, tb, rows_per_step):
    n = h2t.shape[1]
    eb = rows_per_step * N_KEYS
    nblk = N_KEYS // rows_per_step
    scored = lambda e: jnp.minimum(e, nblk - 1)
    combined = lambda e: jnp.maximum(e - 1, 0)
    small = pl.BlockSpec((PEER_HEADS, rows_per_step, tb), lambda i, e: (0, combined(e), i))
    full = pl.BlockSpec((PEER_HEADS, N_KEYS, tb), lambda i, e: (0, 0, i))
    tok = pl.BlockSpec((tb, D_MODEL), lambda i, e: (i, 0))
    return pl.pallas_call(
        functools.partial(_peer_kernel, rows_per_step=rows_per_step),
        out_shape=jax.ShapeDtypeStruct((n, D_MODEL), F32),
        grid=(n // tb, nblk + 1),
        in_specs=[pl.BlockSpec((D_MODEL, tb), lambda i, e: (0, i)),
                  pl.BlockSpec((eb, D_MODEL), lambda i, e: (scored(e), 0)),
                  pl.BlockSpec((D_MODEL, eb), lambda i, e: (0, combined(e))),
                  small, small, full, full, tok,
                  pl.BlockSpec((1, D_MODEL), lambda i, e: (0, 0))],
        out_specs=tok,
        scratch_shapes=[pltpu.VMEM((D_MODEL, tb), F32), pltpu.VMEM((eb, tb), F32),
                        pltpu.VMEM((eb, tb), F32)],
        compiler_params=_params("arbitrary", "arbitrary"),
        name="peer",
    )(h2t, u, vt, pk, ck, qk, r2k, x1, g_final.reshape(1, D_MODEL))


def _layer(x, prm):
    nb, seq, _ = x.shape
    n = nb * seq
    x2d = x.reshape(n, D_MODEL)
    z5 = _in_proj(x2d, prm['g_mix'], prm['w_in'], 256)
    z5s = z5.reshape(z5.shape[0], nb, seq, D_MODEL)
    ya = _lru(z5s, prm['conv_w'], prm['conv_b'], *prm['lru'], rt=256)
    ys = _s5(z5s, *prm['s5'], rb=min(256, seq // S5_CHUNK))
    x1, h2t = _merge(x2d, ya.reshape(n, D_MODEL), ys.reshape(n, D_MODEL), z5,
                     prm['w_lru_out'], prm['w_glu_a'], prm['w_glu_b'], prm['w_out'],
                     prm['s5_d'], prm['g_ffn'], 256)
    pk, ck, qk, r2k = _route(h2t, prm['wqt'], prm['keys'], 512)
    y = _peer(h2t, prm['u'], prm['vt'], pk, ck, qk, r2k, x1, prm['g_final'], 512, 8)
    return y.reshape(nb, seq, D_MODEL)


def kernel(x_prompt, x_sample, g_mix, w_in, conv_w, conv_b, lru_wr, lru_br, lru_wi, lru_bi, lru_lam,
           w_lru_out, s5_lam_re, s5_lam_im, s5_log_step, s5_b_re, s5_b_im, s5_c_re, s5_c_im, s5_d,
           w_glu_a, w_glu_b, w_out, g_ffn, w_query, sub_keys, expert_u, expert_v, g_final):
    depth = g_mix.shape[0]
    xp, xs = x_prompt, x_sample
    for l in range(depth):
        prm = {
            'g_mix': g_mix[l], 'w_in': w_in[l].astype(BF16),
            'conv_w': conv_w[l], 'conv_b': conv_b[l].reshape(1, D_MODEL),
            'lru': _lru_weights(lru_wr[l], lru_br[l], lru_wi[l], lru_bi[l], lru_lam[l]),
            's5': _s5_weights(s5_lam_re[l], s5_lam_im[l], s5_log_step[l], s5_b_re[l], s5_b_im[l],
                              s5_c_re[l], s5_c_im[l]),
            'w_lru_out': w_lru_out[l].astype(BF16), 'w_glu_a': w_glu_a[l].astype(BF16),
            'w_glu_b': w_glu_b[l].astype(BF16), 'w_out': w_out[l].astype(BF16),
            's5_d': s5_d[l], 'g_ffn': g_ffn[l],
            'wqt': w_query[l].T.astype(BF16),
            'keys': sub_keys[l].astype(BF16),
            'u': expert_u[l].astype(BF16), 'vt': expert_v[l].T.astype(BF16),
            'g_final': g_final,
        }
        assert depth == 1
        xp = _layer(xp, prm)
        xs = _layer(xs, prm)
    return (xp, xs)
```

```python
import functools
import math

import jax
import jax.numpy as jnp
from jax import lax
from jax.experimental import pallas as pl
from jax.experimental.pallas import tpu as pltpu

F32 = jnp.float32
BF16 = jnp.bfloat16

LANES = 128
SUBLANES = 8
VMEM_LIMIT_BYTES = 56 * 1024 * 1024

D_MODEL = 1024
LRU_HEADS = 16
LRU_HEAD_DIM = D_MODEL // LRU_HEADS
LRU_C = 8.0
SSM_GROUP = 16
SSM_GROUPS = D_MODEL // SSM_GROUP
SSM_STATE = 64
N_KEYS = 128
PEER_HEADS = 8
PEER_TOPK = 16
PEER_HALF = 128
EPS = 1e-6

N_TILES = D_MODEL // LANES
GROUPS_PER_TILE = LANES // SSM_GROUP
S5_CHUNK = 8
S5_STATE_COLS = 2 * 2 * GROUPS_PER_TILE * SSM_STATE
HALF_STATE = GROUPS_PER_TILE * SSM_STATE


def _gelu(x):
    c = math.sqrt(2.0 / math.pi)
    return 0.5 * x * (1.0 + jnp.tanh(c * (x + 0.044715 * (x * x * x))))


def _rms(x, g):
    ms = jnp.mean(x * x, axis=-1, keepdims=True)
    return x * lax.rsqrt(ms + EPS) * g


def _params(*sem, flags=None):
    return pltpu.CompilerParams(dimension_semantics=sem, vmem_limit_bytes=VMEM_LIMIT_BYTES, flags=flags)


def _in_proj_kernel(x_ref, g_ref, w_ref, z_ref):
    h = _rms(x_ref[...], g_ref[...]).astype(BF16)
    for j in range(z_ref.shape[0]):
        r = jnp.dot(h, w_ref[:, j * D_MODEL:(j + 1) * D_MODEL], preferred_element_type=F32)
        for o in range(N_TILES):
            z_ref[j, o] = r[:, o * LANES:(o + 1) * LANES]


def _in_proj(x2d, g_mix, w_in, tb):
    n = x2d.shape[0]
    nj = w_in.shape[1] // D_MODEL
    return pl.pallas_call(
        _in_proj_kernel,
        out_shape=jax.ShapeDtypeStruct((nj, N_TILES, n, LANES), F32),
        grid=(n // tb,),
        in_specs=[pl.BlockSpec((tb, D_MODEL), lambda i: (i, 0)),
                  pl.BlockSpec((1, D_MODEL), lambda i: (0, 0)),
                  pl.BlockSpec((D_MODEL, nj * D_MODEL), lambda i: (0, 0))],
        out_specs=pl.BlockSpec((nj, N_TILES, tb, LANES), lambda i: (0, 0, i, 0)),
        compiler_params=_params("arbitrary"),
        name="in_proj",
    )(x2d, g_mix.reshape(1, D_MODEL), w_in)


def _group_scan(a, b, reverse):
    row = lax.broadcasted_iota(jnp.int32, a.shape, 0) & (SUBLANES - 1)
    n = a.shape[0]
    for s in (1, 2, 4):
        if reverse:
            a_s = pltpu.roll(a, n - s, 0)
            b_s = pltpu.roll(b, n - s, 0)
            valid = row < SUBLANES - s
        else:
            a_s = pltpu.roll(a, s, 0)
            b_s = pltpu.roll(b, s, 0)
            valid = row >= s
        b = a * jnp.where(valid, b_s, 0.0) + b
        a = a * jnp.where(valid, a_s, 1.0)
    return a, b


def _carry_groups(a, b, h, reverse):
    groups = a.shape[0] // SUBLANES
    outs = [None] * groups
    order = range(groups - 1, -1, -1) if reverse else range(groups)
    edge = 0 if reverse else SUBLANES - 1
    for k in order:
        ak = a[k * SUBLANES:(k + 1) * SUBLANES]
        bk = b[k * SUBLANES:(k + 1) * SUBLANES]
        outs[k] = ak * h + bk
        h = ak[edge:edge + 1] * h + bk[edge:edge + 1]
    return jnp.concatenate(outs, axis=0), h


def _lru_kernel(xa_ref, ga_ref, cw_ref, cb_ref, wgf_ref, wgb_ref, bgf_ref, bgb_ref, sp_ref,
                o_ref, hf_ref, hb_ref, xc_ref, *, rt):
    seq = xa_ref.shape[0]
    nt = seq // rt
    cw = cw_ref[...]
    cb = cb_ref[...]

    def conv_tile(j, c):
        r0 = pl.multiple_of(j * rt, rt)
        cur = xa_ref[pl.ds(r0, rt), :]
        p0 = pl.multiple_of(jnp.maximum(r0 - SUBLANES, 0), SUBLANES)
        n0 = pl.multiple_of(jnp.minimum(r0 + rt, seq - SUBLANES), SUBLANES)
        prev = jnp.where(j > 0, xa_ref[pl.ds(p0, SUBLANES), :], 0.0)
        nxt = jnp.where(j < nt - 1, xa_ref[pl.ds(n0, SUBLANES), :], 0.0)
        ext = jnp.concatenate([prev, cur, nxt], axis=0)
        n = rt + 2 * SUBLANES
        xm2 = pltpu.roll(ext, 2, 0)[SUBLANES:SUBLANES + rt]
        xm1 = pltpu.roll(ext, 1, 0)[SUBLANES:SUBLANES + rt]
        xp1 = pltpu.roll(ext, n - 1, 0)[SUBLANES:SUBLANES + rt]
        xc_ref[pl.ds(r0, rt), :] = cw[0:1] * xm2 + cw[1:2] * xm1 + cw[2:3] * cur + cw[3:4] * xp1 + cb
        return c

    lax.fori_loop(0, nt, conv_tile, 0)

    def sigmoid(x):
        return 0.5 * jnp.tanh(0.5 * x) + 0.5

    def direction(j, h, wg_ref, bg_ref, sp, dst_ref, reverse):
        rows = pl.ds(pl.multiple_of(j * rt, rt), rt)
        xc = xc_ref[rows, :]
        gz = jnp.dot(xc.astype(BF16), wg_ref[...], preferred_element_type=F32) + bg_ref[...]
        r = sigmoid(gz[:, :LANES])
        i = sigmoid(gz[:, LANES:])
        log_a = -LRU_C * r * sp
        a = jnp.exp(log_a)
        v = -jnp.tanh(log_a) * (1.0 + a * a)
        b = jnp.where(v > 0.0, v * lax.rsqrt(v), 0.0) * i * xc
        a_g, b_g = _group_scan(a, b, reverse)
        h_tile, h = _carry_groups(a_g, b_g, h, reverse)
        dst_ref[rows, :] = h_tile
        return h

    def body(j, carry):
        hf, hb = carry
        hf = direction(j, hf, wgf_ref, bgf_ref, sp_ref[0:1, :], hf_ref, False)
        hb = direction(nt - 1 - j, hb, wgb_ref, bgb_ref, sp_ref[1:2, :], hb_ref, True)
        return hf, hb

    zero = jnp.zeros((1, LANES), F32)
    lax.fori_loop(0, nt, body, (zero, zero), unroll=2)

    def finish(j, c):
        rows = pl.ds(pl.multiple_of(j * rt, rt), rt)
        y = (hf_ref[rows, :] + hb_ref[rows, :]) * _gelu(ga_ref[rows, :])
        o_ref[rows, :] = y.astype(o_ref.dtype)
        return c

    lax.fori_loop(0, nt, finish, 0)


def _lru(z5, conv_w, conv_b, wgf, wgb, bgf, bgb, sp, rt):
    _, _, nb, seq, _ = z5.shape
    slab = lambda k: pl.BlockSpec((None, None, None, seq, LANES), lambda b, o: (k, o, b, 0, 0))
    per_tile = lambda shape: pl.BlockSpec((None,) + shape, lambda b, o: (o,) + (0,) * len(shape))
    return pl.pallas_call(
        functools.partial(_lru_kernel, rt=rt),
        out_shape=jax.ShapeDtypeStruct((N_TILES, nb, seq, LANES), BF16),
        grid=(nb, N_TILES),
        in_specs=[slab(0), slab(1),
                  pl.BlockSpec((4, LANES), lambda b, o: (0, o)),
                  pl.BlockSpec((1, LANES), lambda b, o: (0, o)),
                  per_tile((LANES, 2 * LANES)), per_tile((LANES, 2 * LANES)),
                  per_tile((1, 2 * LANES)), per_tile((1, 2 * LANES)),
                  pl.BlockSpec((2, LANES), lambda b, o: (0, o))],
        out_specs=pl.BlockSpec((None, None, seq, LANES), lambda b, o: (o, b, 0, 0)),
        scratch_shapes=[pltpu.VMEM((seq, LANES), F32)] * 3,
        compiler_params=_params("arbitrary", "arbitrary"),
        name="lru",
    )(z5, z5, conv_w, conv_b, wgf, wgb, bgf, bgb, sp)


def _lru_weights(lru_wr, lru_br, lru_wi, lru_bi, lru_lam):
    eye = jnp.eye(2, dtype=F32)

    def blockdiag(w):
        w = w.reshape(N_TILES, 2, LRU_HEAD_DIM, LRU_HEAD_DIM)
        return jnp.einsum('ohij,hk->ohikj', w, eye).reshape(N_TILES, LANES, LANES)

    def direction(d):
        wg = jnp.concatenate([blockdiag(lru_wr[d]), blockdiag(lru_wi[d])], axis=-1).astype(BF16)
        bg = jnp.concatenate([lru_br[d].reshape(N_TILES, 1, LANES),
                              lru_bi[d].reshape(N_TILES, 1, LANES)], axis=-1)
        return wg, bg

    wgf, bgf = direction(0)
    wgb, bgb = direction(1)
    return wgf, wgb, bgf, bgb, jax.nn.softplus(-lru_lam)


def _cmul(ar, ai, br, bi):
    return ar * br - ai * bi, ar * bi + ai * br


def _s5_kernel(u_ref, wi_ref, wp_ref, wq_ref, tab_ref, o_ref, xcat_ref, st_ref, *, rb):
    seq = u_ref.shape[0]
    nc = seq // S5_CHUNK
    nblk = nc // rb
    ntile = nc // SUBLANES

    for p in range(S5_CHUNK):
        xcat_ref[:, p * LANES:(p + 1) * LANES] = u_ref[pl.ds(p, nc, stride=S5_CHUNK), :].astype(BF16)

    for k in range(nblk):
        rows = pl.ds(k * rb, rb)
        st_ref[rows, :] = jnp.dot(xcat_ref[rows, :], wp_ref[...], preferred_element_type=F32)

    row = lax.broadcasted_iota(jnp.int32, (SUBLANES, HALF_STATE), 0)

    def tile_scan(t, er, ei, d, reverse):
        rows = pl.ds(pl.multiple_of(t * SUBLANES, SUBLANES), SUBLANES)
        c0 = d * 2 * HALF_STATE
        xr = st_ref[rows, c0:c0 + HALF_STATE]
        xi = st_ref[rows, c0 + HALF_STATE:c0 + 2 * HALF_STATE]
        for m, s in enumerate((1, 2, 4)):
            shift = SUBLANES - s if reverse else s
            valid = (row < SUBLANES - s) if reverse else (row >= s)
            sr = jnp.where(valid, pltpu.roll(xr, shift, 0), 0.0)
            si = jnp.where(valid, pltpu.roll(xi, shift, 0), 0.0)
            pr = tab_ref[d, 0, SUBLANES * m:SUBLANES * (m + 1), :]
            pi = tab_ref[d, 1, SUBLANES * m:SUBLANES * (m + 1), :]
            mr, mi = _cmul(pr, pi, sr, si)
            xr = xr + mr
            xi = xi + mi
        shift = SUBLANES - 1 if reverse else 1
        valid = (row < SUBLANES - 1) if reverse else (row >= 1)
        qr = jnp.where(valid, pltpu.roll(xr, shift, 0), 0.0)
        qi = jnp.where(valid, pltpu.roll(xi, shift, 0), 0.0)
        wr = tab_ref[d, 0, 4 * SUBLANES:5 * SUBLANES, :]
        wi = tab_ref[d, 1, 4 * SUBLANES:5 * SUBLANES, :]
        cr, ci = _cmul(wr, wi, er, ei)
        st_ref[rows, c0:c0 + HALF_STATE] = qr + cr
        st_ref[rows, c0 + HALF_STATE:c0 + 2 * HALF_STATE] = qi + ci
        edge = 0 if reverse else SUBLANES - 1
        nr, ni = _cmul(tab_ref[d, 0, 3 * SUBLANES:4 * SUBLANES, :],
                       tab_ref[d, 1, 3 * SUBLANES:4 * SUBLANES, :], er, ei)
        full = (SUBLANES, HALF_STATE)
        return (nr + jnp.broadcast_to(xr[edge:edge + 1], full),
                ni + jnp.broadcast_to(xi[edge:edge + 1], full))

    def body(t, carry):
        fr, fi, br, bi = carry
        fr, fi = tile_scan(t, fr, fi, 0, False)
        br, bi = tile_scan(ntile - 1 - t, br, bi, 1, True)
        return fr, fi, br, bi

    zero = jnp.zeros((SUBLANES, HALF_STATE), F32)
    lax.fori_loop(0, ntile, body, (zero, zero, zero, zero))

    for k in range(nblk):
        rows = pl.ds(k * rb, rb)
        y = jnp.dot(xcat_ref[rows, :], wi_ref[...], preferred_element_type=F32)
        y = y + jnp.dot(st_ref[rows, :].astype(BF16), wq_ref[...], preferred_element_type=F32)
        for p in range(S5_CHUNK):
            o_ref[pl.ds(k * rb * S5_CHUNK + p, rb, stride=S5_CHUNK), :] = y[:, p * LANES:(p + 1) * LANES]


def _s5(z5, wi, wp, wq, tab, rb):
    _, _, nb, seq, _ = z5.shape
    nc = seq // S5_CHUNK
    kdim = S5_CHUNK * LANES
    per_tile = lambda shape: pl.BlockSpec((None,) + shape, lambda o, b: (o,) + (0,) * len(shape))
    return pl.pallas_call(
        functools.partial(_s5_kernel, rb=rb),
        out_shape=jax.ShapeDtypeStruct((N_TILES, nb, seq, LANES), F32),
        grid=(N_TILES, nb),
        in_specs=[pl.BlockSpec((None, None, None, seq, LANES), lambda o, b: (2, o, b, 0, 0)),
                  per_tile((kdim, kdim)), per_tile((kdim, S5_STATE_COLS)),
                  per_tile((S5_STATE_COLS, kdim)), per_tile((2, 2, 5 * SUBLANES, HALF_STATE))],
        out_specs=pl.BlockSpec((None, None, seq, LANES), lambda o, b: (o, b, 0, 0)),
        scratch_shapes=[pltpu.VMEM((nc, kdim), BF16), pltpu.VMEM((nc, S5_STATE_COLS), F32)],
        compiler_params=_params("arbitrary", "arbitrary"),
        name="s5",
    )(z5, wi, wp, wq, tab)


def _s5_weights(lam_re, lam_im, log_step, b_re, b_im, c_re, c_im):
    T = S5_CHUNK
    G, P, H = SSM_GROUPS, SSM_STATE, SSM_GROUP
    dt = jnp.exp(log_step)[:, :, None, None]
    kk = jnp.arange(8 * T + 1, dtype=F32)[None, None, :, None]
    mag = jnp.exp(lam_re[:, :, None, :] * dt * kk)
    ang = lam_im[:, :, None, :] * dt * kk
    pr = mag * jnp.cos(ang)
    pi = mag * jnp.sin(ang)
    ar, ai = pr[:, :, 1], pi[:, :, 1]
    den = lam_re * lam_re + lam_im * lam_im
    nr = ar - 1.0
    cr = (nr * lam_re + ai * lam_im) / den
    ci = (ai * lam_re - nr * lam_im) / den
    bbr = cr[..., None] * b_re - ci[..., None] * b_im
    bbi = cr[..., None] * b_im + ci[..., None] * b_re
    mr = pr[:, :, :T, :, None] * bbr[:, :, None] - pi[:, :, :T, :, None] * bbi[:, :, None]
    mi = pr[:, :, :T, :, None] * bbi[:, :, None] + pi[:, :, :T, :, None] * bbr[:, :, None]
    taps = jnp.einsum('gdon,gdknh->gdkoh', c_re, mr) - jnp.einsum('gdon,gdknh->gdkoh', c_im, mi)
    pos = jnp.arange(T)
    lag = pos[None, :] - pos[:, None]
    kf = taps[:, 0][:, jnp.clip(lag, 0, T - 1)] * (lag >= 0)[None, :, :, None, None]
    kb = taps[:, 1][:, jnp.clip(-lag, 0, T - 1)] * (lag <= 0)[None, :, :, None, None]
    intra = (kf + kb).transpose(0, 1, 4, 2, 3)
    eye = jnp.eye(GROUPS_PER_TILE, dtype=F32)
    intra = intra.reshape(N_TILES, GROUPS_PER_TILE, T, H, T, H)
    wi = jnp.einsum('ogphqk,gj->opghqjk', intra, eye).reshape(N_TILES, T * LANES, T * LANES)

    def pmat(m):
        f = m[:, 0, ::-1]
        b = m[:, 1]
        return jnp.stack([f, b], axis=1)
    pm = jnp.stack([pmat(mr), pmat(mi)], axis=2)
    pm = pm.transpose(0, 3, 5, 1, 2, 4).reshape(N_TILES, GROUPS_PER_TILE, T, H, 2, 2, P)
    wp = jnp.einsum('ogphdcn,gj->opghdcjn', pm, eye).reshape(N_TILES, T * LANES, S5_STATE_COLS)

    lag_f = pos + 1
    lag_b = T - pos
    def qpair(d, lags):
        prd = pr[:, d][:, lags]
        pid = pi[:, d][:, lags]
        on_re = c_re[:, d][:, None] * prd[:, :, None, :] - c_im[:, d][:, None] * pid[:, :, None, :]
        on_im = -(c_re[:, d][:, None] * pid[:, :, None, :] + c_im[:, d][:, None] * prd[:, :, None, :])
        return jnp.stack([on_re, on_im], axis=1)
    qm = jnp.stack([qpair(0, lag_f), qpair(1, lag_b)], axis=1)
    qm = qm.transpose(0, 1, 2, 5, 3, 4).reshape(N_TILES, GROUPS_PER_TILE, 2, 2, P, T, H)
    wq = jnp.einsum('ogdcnph,gj->odcgnpjh', qm, eye).reshape(N_TILES, S5_STATE_COLS, T * LANES)

    def lanes(x):
        x = x.reshape(N_TILES, GROUPS_PER_TILE, 2, x.shape[2], P)
        return x.transpose(0, 2, 3, 1, 4).reshape(N_TILES, 2, x.shape[3], HALF_STATE)
    steps = jnp.array([T, 2 * T, 4 * T, 8 * T])
    rows_f = T * jnp.arange(SUBLANES)
    rows_b = T * (SUBLANES - 1 - jnp.arange(SUBLANES))
    def table(p):
        head = jnp.repeat(p[:, :, steps], SUBLANES, axis=2)
        tail = jnp.stack([p[:, 0][:, rows_f], p[:, 1][:, rows_b]], axis=1)
        return lanes(jnp.concatenate([head, tail], axis=2))
    tab = jnp.stack([table(pr), table(pi)], axis=2)
    return wi.astype(BF16), wp.astype(BF16), wq.astype(BF16), tab


def _merge_kernel(x_ref, ya_ref, ys_ref, ub_ref, g1_ref, g2_ref, wlo_ref, wga_ref, wgb_ref, wo_ref,
                  d_ref, gf_ref, x1_ref, h2t_ref):
    wide = lambda ref: jnp.concatenate([ref[o] for o in range(N_TILES)], axis=1)
    ya = jnp.dot(wide(ya_ref), wlo_ref[...], preferred_element_type=F32)
    yg = _gelu(wide(ys_ref) + d_ref[...] * wide(ub_ref)).astype(BF16)
    yb = (jnp.dot(yg, wga_ref[...], preferred_element_type=F32)
          * jax.nn.sigmoid(jnp.dot(yg, wgb_ref[...], preferred_element_type=F32)))
    merged = jax.nn.sigmoid(wide(g1_ref)) * ya + jax.nn.sigmoid(wide(g2_ref)) * yb
    x1 = x_ref[...] + jnp.dot(merged.astype(BF16), wo_ref[...], preferred_element_type=F32)
    x1_ref[...] = x1
    h2t_ref[...] = _rms(x1, gf_ref[...]).T.astype(BF16)


def _merge(x2d, ya, ys, z5f, w_lru_out, w_glu_a, w_glu_b, w_out, s5_d, g_ffn, tb):
    n = x2d.shape[0]
    tok = pl.BlockSpec((tb, D_MODEL), lambda i: (i, 0))
    slabs = pl.BlockSpec((N_TILES, tb, LANES), lambda i: (0, i, 0))
    zsl = lambda k: pl.BlockSpec((None, N_TILES, tb, LANES), lambda i: (k, 0, i, 0))
    wsp = pl.BlockSpec((D_MODEL, D_MODEL), lambda i: (0, 0))
    vec = pl.BlockSpec((1, D_MODEL), lambda i: (0, 0))
    return pl.pallas_call(
        _merge_kernel,
        out_shape=(jax.ShapeDtypeStruct((n, D_MODEL), F32), jax.ShapeDtypeStruct((D_MODEL, n), BF16)),
        grid=(n // tb,),
        in_specs=[tok, slabs, slabs, zsl(2), zsl(3), zsl(4), wsp, wsp, wsp, wsp, vec, vec],
        out_specs=(tok, pl.BlockSpec((D_MODEL, tb), lambda i: (0, i))),
        compiler_params=_params("arbitrary"),
        name="merge",
    )(x2d, ya, ys, z5f, z5f, z5f, w_lru_out, w_glu_a, w_glu_b, w_out,
      s5_d.reshape(1, D_MODEL), g_ffn.reshape(1, D_MODEL))


def _top16(s):
    keys = lax.broadcasted_iota(jnp.int32, s.shape, 0)
    slot = lax.broadcasted_iota(jnp.int32, (PEER_TOPK, s.shape[1]), 0)

    def body(k, carry):
        s, rank, vals = carry
        m = jnp.max(s, axis=0, keepdims=True)
        first = jnp.min(jnp.where(s == m, keys, N_KEYS), axis=0, keepdims=True)
        sel = keys == first
        s = jnp.where(sel, -jnp.inf, s)
        rank = jnp.where(sel, k, rank)
        vals = jnp.where(slot == k, m, vals)
        return s, rank, vals

    init = (s, jnp.full(s.shape, PEER_TOPK, jnp.int32), jnp.zeros((PEER_TOPK, s.shape[1]), F32))
    _, rank, vals = lax.fori_loop(0, PEER_TOPK, body, init)
    return vals, rank


def _pair_select(v1, v2):
    w = v1.shape[1]
    r8 = lax.broadcasted_iota(jnp.int32, (SUBLANES, w), 0)
    neg = -jnp.inf
    tiles = []

    big = PEER_TOPK * PEER_TOPK

    def fixed_a(a, b0, limit):
        sums = v1[a:a + 1] + v2[b0:b0 + SUBLANES]
        b = r8 + b0
        ok = b < limit
        tiles.append((jnp.where(ok, sums, neg), jnp.where(ok, a * PEER_TOPK + b, big), a, None))

    def fixed_b(b, a0, lo, hi):
        sums = v1[a0:a0 + SUBLANES] + v2[b:b + 1]
        a = r8 + a0
        ok = jnp.where(a >= lo, a, hi) < hi
        tiles.append((jnp.where(ok, sums, neg), jnp.where(ok, a * PEER_TOPK + b, big), None, a0))

    fixed_a(0, 0, 16); fixed_a(0, 8, 16); fixed_a(1, 0, 8); fixed_a(2, 0, 5); fixed_a(3, 0, 4)
    fixed_b(0, 0, 4, 16); fixed_b(0, 8, 4, 16); fixed_b(1, 0, 4, 8); fixed_b(2, 0, 4, 5)

    top = v1[0:1] + v2[0:1]
    sums = [t[0] for t in tiles]
    picked = [jnp.zeros((SUBLANES, w), F32) for _ in tiles]
    for _ in range(PEER_TOPK):
        m = functools.reduce(jnp.maximum, sums)
        m = jnp.max(m, axis=0, keepdims=True)
        cand = [jnp.where(s == m, t[1], big) for s, t in zip(sums, tiles)]
        first = jnp.min(functools.reduce(jnp.minimum, cand), axis=0, keepdims=True)
        for n, t in enumerate(tiles):
            sel = t[1] == first
            picked[n] = jnp.where(sel, 1.0, picked[n])
            sums[n] = jnp.where(sel, neg, sums[n])

    r16 = lax.broadcasted_iota(jnp.int32, (PEER_TOPK, w), 0)
    cnt = jnp.zeros((PEER_TOPK, w), F32)
    den = jnp.zeros((SUBLANES, w), F32)
    lo_rows = jnp.zeros((SUBLANES, w), F32)
    hi_rows = jnp.zeros((SUBLANES, w), F32)
    for n, t in enumerate(tiles):
        den = den + picked[n] * jnp.exp(jnp.where(picked[n] > 0, t[0], top) - top)
        if t[2] is not None:
            cnt = cnt + jnp.where(r16 == t[2], jnp.sum(picked[n], axis=0, keepdims=True), 0.0)
        elif t[3] == 0:
            lo_rows = lo_rows + picked[n]
        else:
            hi_rows = hi_rows + picked[n]
    cnt = cnt + jnp.concatenate([lo_rows, hi_rows], axis=0)
    return cnt, jnp.sum(den, axis=0, keepdims=True)


def _sort_pairs(n):
    pairs = []

    def merge(lo, m, r):
        step = 2 * r
        if step < m:
            merge(lo, m, step)
            merge(lo + r, m, step)
            pairs.extend((i, i + r) for i in range(lo + r, lo + m - r, step))
        else:
            pairs.append((lo, lo + r))

    def sort(lo, m):
        if m > 1:
            sort(lo, m // 2)
            sort(lo + m // 2, m // 2)
            merge(lo, m, 1)

    sort(0, n)
    return pairs


_SORT16 = _sort_pairs(PEER_TOPK)


def _exchange(v, i, j):
    a, b = v[i], v[j]
    if b is None:
        return
    if a is None:
        v[i], v[j] = b, None
        return
    v[i], v[j] = jnp.maximum(a, b), jnp.minimum(a, b)


def _sorted_top16(tiles):
    v = list(tiles) + [None] * (PEER_TOPK - len(tiles))
    for i, j in _SORT16:
        _exchange(v, i, j)
    for s in (1, 2, 4):
        r = [None if x is None else pltpu.roll(x, s, 0) for x in v]
        merged = []
        for k in range(PEER_TOPK):
            a, b = v[k], r[PEER_TOPK - 1 - k]
            merged.append(b if a is None else a if b is None else jnp.maximum(a, b))
        v = merged
        for d in (8, 4, 2, 1):
            for i in range(PEER_TOPK):
                if not i & d:
                    _exchange(v, i, i + d)
    return v


def _dup_words(x):
    bits = pltpu.bitcast(x.astype(BF16).astype(F32), jnp.uint32)
    return bits | (bits >> 16)


def _rows_bf16(words, rows):
    return pltpu.bitcast(jnp.broadcast_to(words, (rows // 2, words.shape[1])), BF16)


def _route_fast(a1, a2):
    w = a1.shape[1]
    split = lambda a: [a[SUBLANES * k:SUBLANES * (k + 1)] for k in range(N_KEYS // SUBLANES)]
    t1 = _sorted_top16(split(a1))
    t2 = _sorted_top16(split(a2))
    row = lax.broadcasted_iota(jnp.int32, (SUBLANES, w), 0)

    def column(t, base):
        out = t[base]
        for r in range(1, SUBLANES):
            out = jnp.where(row == r, t[base + r], out)
        return out

    v1 = [column(t1, 0), column(t1, SUBLANES)]
    v2 = [column(t2, 0), column(t2, SUBLANES)]
    neg = -jnp.inf
    cands = [t1[0] + v2[0], t1[0] + v2[1], t1[1] + v2[0],
             jnp.where(row < 5, t1[2] + v2[0], neg), jnp.where(row < 4, t1[3] + v2[0], neg),
             jnp.where(row >= 4, v1[0] + t2[0], neg), v1[1] + t2[0],
             jnp.where(row >= 4, v1[0] + t2[1], neg), jnp.where(row == 4, v1[0] + t2[2], neg)]
    theta = _sorted_top16(cands)[PEER_TOPK - 1][0:1]
    top = t1[0] + t2[0]
    n_pairs = jnp.zeros((SUBLANES, w), F32)
    den = jnp.zeros((SUBLANES, w), F32)
    for c in cands:
        hit = c >= theta
        n_pairs = n_pairs + jnp.where(hit, 1.0, 0.0)
        den = den + jnp.where(hit, jnp.exp(jnp.where(hit, c, top) - top), 0.0)
    n_pairs = jnp.sum(n_pairs, axis=0, keepdims=True)
    inv_den = 1.0 / jnp.sum(den, axis=0, keepdims=True)

    in1 = a1 >= t1[PEER_TOPK - 1][0:1]
    in2 = a2 >= t2[PEER_TOPK - 1][0:1]
    n1 = jnp.sum(jnp.where(in1, 1.0, 0.0), axis=0, keepdims=True)
    n2 = jnp.sum(jnp.where(in2, 1.0, 0.0), axis=0, keepdims=True)
    ck = jnp.zeros(a1.shape, F32)
    r2 = jnp.zeros(a2.shape, F32)
    for b in range(PEER_TOPK):
        ck = ck + jnp.where(a1 + t2[b][0:1] >= theta, 1.0, 0.0)
        r2 = r2 + jnp.where(t2[b][0:1] > a2, 1.0, 0.0)
    ck = jnp.where(in1, ck, 0.0)
    pk = jnp.where(in1, jnp.exp(jnp.where(in1, a1, 0.0) - t1[0][0:1]), 0.0)
    qk = jnp.where(in2, jnp.exp(jnp.where(in2, a2, 0.0) - t2[0][0:1]) * inv_den, 0.0)
    k = float(PEER_TOPK)
    flag = jnp.where((n1 != k) | (n2 != k) | (n_pairs != k), 1.0, 0.0)
    return pk, ck, qk, r2, flag


def _route_exact(a1, a2):
    v1, rank1 = _top16(a1)
    v2, rank2 = _top16(a2)
    cnt, den = _pair_select(v1, v2)
    in1 = rank1 < PEER_TOPK
    in2 = rank2 < PEER_TOPK
    ck = jnp.zeros(a1.shape, F32)
    for a in range(PEER_TOPK):
        ck = jnp.where(rank1 == a, cnt[a:a + 1], ck)
    pk = jnp.where(in1, jnp.exp(jnp.where(in1, a1, v1[0:1]) - v1[0:1]), 0.0)
    qk = jnp.where(in2, jnp.exp(jnp.where(in2, a2, v2[0:1]) - v2[0:1]) / den, 0.0)
    return pk, ck, qk, rank2.astype(F32)


def _route_kernel(h2t_ref, wqt_ref, keys_ref, p_ref, c_ref, q_ref, r2_ref, s_ref):
    qt = jnp.dot(wqt_ref[...], h2t_ref[...], preferred_element_type=F32)
    s_ref[0] = jnp.dot(keys_ref[0], qt[:PEER_HALF].astype(BF16), preferred_element_type=F32)
    s_ref[1] = jnp.dot(keys_ref[1], qt[PEER_HALF:].astype(BF16), preferred_element_type=F32)
    ntile = s_ref.shape[2] // LANES

    def store(cols, pk, ck, qk, r2):
        p_ref[:, cols] = _dup_words(pk)
        c_ref[:, cols] = _dup_words(ck)
        q_ref[:, cols] = qk.astype(q_ref.dtype)
        r2_ref[:, cols] = r2.astype(r2_ref.dtype)

    flag = jnp.zeros((1, LANES), F32)
    for lt in range(ntile):
        cols = slice(lt * LANES, (lt + 1) * LANES)
        pk, ck, qk, r2, f = _route_fast(s_ref[0, :, cols], s_ref[1, :, cols])
        store(cols, pk, ck, qk, r2)
        flag = jnp.maximum(flag, f)

    @pl.when(jnp.max(flag) > 0.0)
    def _():
        for lt in range(ntile):
            cols = slice(lt * LANES, (lt + 1) * LANES)
            store(cols, *_route_exact(s_ref[0, :, cols], s_ref[1, :, cols]))


def _route(h2t, wqt, keys, tb):
    n = h2t.shape[1]
    words = jax.ShapeDtypeStruct((PEER_HEADS, N_KEYS, n), jnp.uint32)
    halfs = jax.ShapeDtypeStruct((PEER_HEADS, N_KEYS, n), BF16)
    osp = pl.BlockSpec((None, N_KEYS, tb), lambda i, h: (h, 0, i))
    return pl.pallas_call(
        _route_kernel,
        out_shape=(words, words, halfs, halfs),
        grid=(n // tb, PEER_HEADS),
        in_specs=[pl.BlockSpec((D_MODEL, tb), lambda i, h: (0, i)),
                  pl.BlockSpec((2 * PEER_HALF, D_MODEL), lambda i, h: (h, 0)),
                  pl.BlockSpec((None, 2, N_KEYS, PEER_HALF), lambda i, h: (h, 0, 0, 0))],
        out_specs=(osp, osp, osp, osp),
        scratch_shapes=[pltpu.VMEM((2, N_KEYS, tb), F32)],
        compiler_params=_params("arbitrary", "arbitrary"),
        name="route",
    )(h2t, wqt, keys)


def _peer_kernel(h2t_ref, u_ref, vt_ref, p_ref, c_ref, q_ref, r2_ref, x1_ref, g_ref, y_ref, acc_ref,
                 act_a, act_b, *, rows_per_step):
    e = pl.program_id(1)
    last = pl.num_programs(1) - 1

    def score(dst_ref):
        dst_ref[...] = jnp.dot(u_ref[...], h2t_ref[...], preferred_element_type=F32)

    def combine(src_ref):
        parts = []
        for ii in range(rows_per_step):
            gate = None
            for h in range(PEER_HEADS):
                q = q_ref[h]
                hit = r2_ref[h] < _rows_bf16(c_ref[h, ii:ii + 1, :], N_KEYS)
                term = jnp.where(hit, q, jnp.zeros_like(q)) * _rows_bf16(p_ref[h, ii:ii + 1, :], N_KEYS)
                gate = term if gate is None else gate + term
            a = src_ref[ii * N_KEYS:(ii + 1) * N_KEYS, :].astype(BF16)
            parts.append(gate * _gelu(a))
        wt = jnp.concatenate(parts, axis=0)
        acc_ref[...] += jnp.dot(vt_ref[...], wt, preferred_element_type=F32)

    @pl.when(e == 0)
    def _():
        acc_ref[...] = jnp.zeros_like(acc_ref)
        score(act_a)

    @pl.when((e % 2 == 1) & (e < last))
    def _():
        score(act_b)
        combine(act_a)

    @pl.when((e % 2 == 0) & (e > 0) & (e < last))
    def _():
        score(act_a)
        combine(act_b)

    @pl.when(e == last)
    def _():
        combine(act_b if (N_KEYS // rows_per_step) % 2 == 0 else act_a)
        x2 = x1_ref[...] + acc_ref[...].T
        y_ref[...] = _rms(x2, g_ref[...])


def _peer(h2t, u, vt, pk, ck, qk, r2k, x1, g_final, tb, rows_per_step):
    n = h2t.shape[1]
    eb = rows_per_step * N_KEYS
    nblk = N_KEYS // rows_per_step
    scored = lambda e: jnp.minimum(e, nblk - 1)
    combined = lambda e: jnp.maximum(e - 1, 0)
    small = pl.BlockSpec((PEER_HEADS, rows_per_step, tb), lambda i, e: (0, combined(e), i))
    full = pl.BlockSpec((PEER_HEADS, N_KEYS, tb), lambda i, e: (0, 0, i))
    tok = pl.BlockSpec((tb, D_MODEL), lambda i, e: (i, 0))
    return pl.pallas_call(
        functools.partial(_peer_kernel, rows_per_step=rows_per_step),
        out_shape=jax.ShapeDtypeStruct((n, D_MODEL), F32),
        grid=(n // tb, nblk + 1),
        in_specs=[pl.BlockSpec((D_MODEL, tb), lambda i, e: (0, i)),
                  pl.BlockSpec((eb, D_MODEL), lambda i, e: (scored(e), 0)),
                  pl.BlockSpec((D_MODEL, eb), lambda i, e: (0, combined(e))),
                  small, small, full, full, tok,
                  pl.BlockSpec((1, D_MODEL), lambda i, e: (0, 0))],
        out_specs=tok,
        scratch_shapes=[pltpu.VMEM((D_MODEL, tb), F32), pltpu.VMEM((eb, tb), F32),
                        pltpu.VMEM((eb, tb), F32)],
        compiler_params=_params("arbitrary", "arbitrary"),
        name="peer",
    )(h2t, u, vt, pk, ck, qk, r2k, x1, g_final.reshape(1, D_MODEL))


def _layer(x, prm):
    nb, seq, _ = x.shape
    n = nb * seq
    x2d = x.reshape(n, D_MODEL)
    z5 = _in_proj(x2d, prm['g_mix'], prm['w_in'], 256)
    z5s = z5.reshape(z5.shape[0], N_TILES, nb, seq, LANES)
    ya = _lru(z5s, prm['conv_w'], prm['conv_b'], *prm['lru'], rt=256)
    ys = _s5(z5s, *prm['s5'], rb=min(256, seq // S5_CHUNK))
    x1, h2t = _merge(x2d, ya.reshape(N_TILES, n, LANES), ys.reshape(N_TILES, n, LANES), z5,
                     prm['w_lru_out'], prm['w_glu_a'], prm['w_glu_b'], prm['w_out'],
                     prm['s5_d'], prm['g_ffn'], 256)
    pk, ck, qk, r2k = _route(h2t, prm['wqt'], prm['keys'], min(1024, n))
    y = _peer(h2t, prm['u'], prm['vt'], pk, ck, qk, r2k, x1, prm['g_final'], 256, 32)
    return y.reshape(nb, seq, D_MODEL)


def kernel(x_prompt, x_sample, g_mix, w_in, conv_w, conv_b, lru_wr, lru_br, lru_wi, lru_bi, lru_lam,
           w_lru_out, s5_lam_re, s5_lam_im, s5_log_step, s5_b_re, s5_b_im, s5_c_re, s5_c_im, s5_d,
           w_glu_a, w_glu_b, w_out, g_ffn, w_query, sub_keys, expert_u, expert_v, g_final):
    depth = g_mix.shape[0]
    xp, xs = x_prompt, x_sample
    for l in range(depth):
        prm = {
            'g_mix': g_mix[l], 'w_in': w_in[l].astype(BF16),
            'conv_w': conv_w[l], 'conv_b': conv_b[l].reshape(1, D_MODEL),
            'lru': _lru_weights(lru_wr[l], lru_br[l], lru_wi[l], lru_bi[l], lru_lam[l]),
            's5': _s5_weights(s5_lam_re[l], s5_lam_im[l], s5_log_step[l], s5_b_re[l], s5_b_im[l],
                              s5_c_re[l], s5_c_im[l]),
            'w_lru_out': w_lru_out[l].astype(BF16), 'w_glu_a': w_glu_a[l].astype(BF16),
            'w_glu_b': w_glu_b[l].astype(BF16), 'w_out': w_out[l].astype(BF16),
            's5_d': s5_d[l], 'g_ffn': g_ffn[l],
            'wqt': w_query[l].T.astype(BF16),
            'keys': sub_keys[l].astype(BF16),
            'u': expert_u[l].astype(BF16), 'vt': expert_v[l].T.astype(BF16),
            'g_final': g_final,
        }
        assert depth == 1
        xp = _layer(xp, prm)
        xs = _layer(xs, prm)
    return (xp, xs)
```

```python
import functools
import math

import jax
import jax.numpy as jnp
from jax import lax
from jax.experimental import pallas as pl
from jax.experimental.pallas import tpu as pltpu

F32 = jnp.float32
BF16 = jnp.bfloat16

LANES = 128
SUBLANES = 8
VMEM_LIMIT_BYTES = 56 * 1024 * 1024

D_MODEL = 1024
LRU_HEADS = 16
LRU_HEAD_DIM = D_MODEL // LRU_HEADS
LRU_C = 8.0
SSM_GROUP = 16
SSM_GROUPS = D_MODEL // SSM_GROUP
SSM_STATE = 64
N_KEYS = 128
PEER_HEADS = 8
PEER_TOPK = 16
PEER_HALF = 128
EPS = 1e-6

N_TILES = D_MODEL // LANES
GROUPS_PER_TILE = LANES // SSM_GROUP
S5_CHUNK = 8
S5_STATE_COLS = 2 * 2 * GROUPS_PER_TILE * SSM_STATE
HALF_STATE = GROUPS_PER_TILE * SSM_STATE


def _gelu(x):
    c = math.sqrt(2.0 / math.pi)
    return 0.5 * x * (1.0 + jnp.tanh(c * (x + 0.044715 * (x * x * x))))


def _rms(x, g):
    ms = jnp.mean(x * x, axis=-1, keepdims=True)
    return x * lax.rsqrt(ms + EPS) * g


def _params(*sem, flags=None):
    return pltpu.CompilerParams(dimension_semantics=sem, vmem_limit_bytes=VMEM_LIMIT_BYTES, flags=flags)


def _in_proj_kernel(x_ref, g_ref, w_ref, z_ref):
    h = _rms(x_ref[...], g_ref[...]).astype(BF16)
    for j in range(z_ref.shape[0]):
        r = jnp.dot(h, w_ref[:, j * D_MODEL:(j + 1) * D_MODEL], preferred_element_type=F32)
        for o in range(N_TILES):
            z_ref[j, o] = r[:, o * LANES:(o + 1) * LANES]


def _in_proj(x2d, g_mix, w_in, tb):
    n = x2d.shape[0]
    nj = w_in.shape[1] // D_MODEL
    return pl.pallas_call(
        _in_proj_kernel,
        out_shape=jax.ShapeDtypeStruct((nj, N_TILES, n, LANES), F32),
        grid=(n // tb,),
        in_specs=[pl.BlockSpec((tb, D_MODEL), lambda i: (i, 0)),
                  pl.BlockSpec((1, D_MODEL), lambda i: (0, 0)),
                  pl.BlockSpec((D_MODEL, nj * D_MODEL), lambda i: (0, 0))],
        out_specs=pl.BlockSpec((nj, N_TILES, tb, LANES), lambda i: (0, 0, i, 0)),
        compiler_params=_params("arbitrary"),
        name="in_proj",
    )(x2d, g_mix.reshape(1, D_MODEL), w_in)


def _group_scan(a, b, reverse):
    row = lax.broadcasted_iota(jnp.int32, a.shape, 0) & (SUBLANES - 1)
    n = a.shape[0]
    for s in (1, 2, 4):
        if reverse:
            a_s = pltpu.roll(a, n - s, 0)
            b_s = pltpu.roll(b, n - s, 0)
            valid = row < SUBLANES - s
        else:
            a_s = pltpu.roll(a, s, 0)
            b_s = pltpu.roll(b, s, 0)
            valid = row >= s
        b = a * jnp.where(valid, b_s, 0.0) + b
        a = a * jnp.where(valid, a_s, 1.0)
    return a, b


def _carry_groups(a, b, h, reverse):
    groups = a.shape[0] // SUBLANES
    outs = [None] * groups
    order = range(groups - 1, -1, -1) if reverse else range(groups)
    edge = 0 if reverse else SUBLANES - 1
    for k in order:
        ak = a[k * SUBLANES:(k + 1) * SUBLANES]
        bk = b[k * SUBLANES:(k + 1) * SUBLANES]
        outs[k] = ak * h + bk
        h = ak[edge:edge + 1] * h + bk[edge:edge + 1]
    return jnp.concatenate(outs, axis=0), h


def _lru_kernel(xa_ref, ga_ref, cw_ref, cb_ref, wgf_ref, wgb_ref, bgf_ref, bgb_ref, sp_ref,
                o_ref, hf_ref, hb_ref, xc_ref, *, rt):
    seq = xa_ref.shape[0]
    nt = seq // rt
    cw = cw_ref[...]
    cb = cb_ref[...]

    def conv_tile(j, c):
        r0 = pl.multiple_of(j * rt, rt)
        cur = xa_ref[pl.ds(r0, rt), :]
        p0 = pl.multiple_of(jnp.maximum(r0 - SUBLANES, 0), SUBLANES)
        n0 = pl.multiple_of(jnp.minimum(r0 + rt, seq - SUBLANES), SUBLANES)
        prev = jnp.where(j > 0, xa_ref[pl.ds(p0, SUBLANES), :], 0.0)
        nxt = jnp.where(j < nt - 1, xa_ref[pl.ds(n0, SUBLANES), :], 0.0)
        ext = jnp.concatenate([prev, cur, nxt], axis=0)
        n = rt + 2 * SUBLANES
        xm2 = pltpu.roll(ext, 2, 0)[SUBLANES:SUBLANES + rt]
        xm1 = pltpu.roll(ext, 1, 0)[SUBLANES:SUBLANES + rt]
        xp1 = pltpu.roll(ext, n - 1, 0)[SUBLANES:SUBLANES + rt]
        xc_ref[pl.ds(r0, rt), :] = cw[0:1] * xm2 + cw[1:2] * xm1 + cw[2:3] * cur + cw[3:4] * xp1 + cb
        return c

    lax.fori_loop(0, nt, conv_tile, 0)

    def sigmoid(x):
        return 0.5 * jnp.tanh(0.5 * x) + 0.5

    def direction(j, h, wg_ref, bg_ref, sp, dst_ref, reverse):
        rows = pl.ds(pl.multiple_of(j * rt, rt), rt)
        xc = xc_ref[rows, :]
        gz = jnp.dot(xc.astype(BF16), wg_ref[...], preferred_element_type=F32) + bg_ref[...]
        r = sigmoid(gz[:, :LANES])
        i = sigmoid(gz[:, LANES:])
        log_a = -LRU_C * r * sp
        a = jnp.exp(log_a)
        v = -jnp.tanh(log_a) * (1.0 + a * a)
        b = jnp.where(v > 0.0, v * lax.rsqrt(v), 0.0) * i * xc
        a_g, b_g = _group_scan(a, b, reverse)
        h_tile, h = _carry_groups(a_g, b_g, h, reverse)
        dst_ref[rows, :] = h_tile
        return h

    def body(j, carry):
        hf, hb = carry
        hf = direction(j, hf, wgf_ref, bgf_ref, sp_ref[0:1, :], hf_ref, False)
        hb = direction(nt - 1 - j, hb, wgb_ref, bgb_ref, sp_ref[1:2, :], hb_ref, True)
        return hf, hb

    zero = jnp.zeros((1, LANES), F32)
    lax.fori_loop(0, nt, body, (zero, zero), unroll=2)

    def finish(j, c):
        rows = pl.ds(pl.multiple_of(j * rt, rt), rt)
        y = (hf_ref[rows, :] + hb_ref[rows, :]) * _gelu(ga_ref[rows, :])
        o_ref[rows, :] = y.astype(o_ref.dtype)
        return c

    lax.fori_loop(0, nt, finish, 0)


def _lru(z5, conv_w, conv_b, wgf, wgb, bgf, bgb, sp, rt):
    _, _, nb, seq, _ = z5.shape
    slab = lambda k: pl.BlockSpec((None, None, None, seq, LANES), lambda b, o: (k, o, b, 0, 0))
    per_tile = lambda shape: pl.BlockSpec((None,) + shape, lambda b, o: (o,) + (0,) * len(shape))
    return pl.pallas_call(
        functools.partial(_lru_kernel, rt=rt),
        out_shape=jax.ShapeDtypeStruct((N_TILES, nb, seq, LANES), BF16),
        grid=(nb, N_TILES),
        in_specs=[slab(0), slab(1),
                  pl.BlockSpec((4, LANES), lambda b, o: (0, o)),
                  pl.BlockSpec((1, LANES), lambda b, o: (0, o)),
                  per_tile((LANES, 2 * LANES)), per_tile((LANES, 2 * LANES)),
                  per_tile((1, 2 * LANES)), per_tile((1, 2 * LANES)),
                  pl.BlockSpec((2, LANES), lambda b, o: (0, o))],
        out_specs=pl.BlockSpec((None, None, seq, LANES), lambda b, o: (o, b, 0, 0)),
        scratch_shapes=[pltpu.VMEM((seq, LANES), F32)] * 3,
        compiler_params=_params("arbitrary", "arbitrary"),
        name="lru",
    )(z5, z5, conv_w, conv_b, wgf, wgb, bgf, bgb, sp)


def _lru_weights(lru_wr, lru_br, lru_wi, lru_bi, lru_lam):
    eye = jnp.eye(2, dtype=F32)

    def blockdiag(w):
        w = w.reshape(N_TILES, 2, LRU_HEAD_DIM, LRU_HEAD_DIM)
        return jnp.einsum('ohij,hk->ohikj', w, eye).reshape(N_TILES, LANES, LANES)

    def direction(d):
        wg = jnp.concatenate([blockdiag(lru_wr[d]), blockdiag(lru_wi[d])], axis=-1).astype(BF16)
        bg = jnp.concatenate([lru_br[d].reshape(N_TILES, 1, LANES),
                              lru_bi[d].reshape(N_TILES, 1, LANES)], axis=-1)
        return wg, bg

    wgf, bgf = direction(0)
    wgb, bgb = direction(1)
    return wgf, wgb, bgf, bgb, jax.nn.softplus(-lru_lam)


def _cmul(ar, ai, br, bi):
    return ar * br - ai * bi, ar * bi + ai * br


def _s5_kernel(u_ref, wi_ref, wp_ref, wq_ref, tab_ref, o_ref, xcat_ref, st_ref, *, rb):
    seq = u_ref.shape[0]
    nc = seq // S5_CHUNK
    nblk = nc // rb
    ntile = nc // SUBLANES

    for p in range(S5_CHUNK):
        xcat_ref[:, p * LANES:(p + 1) * LANES] = u_ref[pl.ds(p, nc, stride=S5_CHUNK), :].astype(BF16)

    for k in range(nblk):
        rows = pl.ds(k * rb, rb)
        st_ref[rows, :] = jnp.dot(xcat_ref[rows, :], wp_ref[...], preferred_element_type=F32)

    row = lax.broadcasted_iota(jnp.int32, (SUBLANES, HALF_STATE), 0)

    def tile_scan(t, er, ei, d, reverse):
        rows = pl.ds(pl.multiple_of(t * SUBLANES, SUBLANES), SUBLANES)
        c0 = d * 2 * HALF_STATE
        xr = st_ref[rows, c0:c0 + HALF_STATE]
        xi = st_ref[rows, c0 + HALF_STATE:c0 + 2 * HALF_STATE]
        for m, s in enumerate((1, 2, 4)):
            shift = SUBLANES - s if reverse else s
            valid = (row < SUBLANES - s) if reverse else (row >= s)
            sr = jnp.where(valid, pltpu.roll(xr, shift, 0), 0.0)
            si = jnp.where(valid, pltpu.roll(xi, shift, 0), 0.0)
            pr = tab_ref[d, 0, SUBLANES * m:SUBLANES * (m + 1), :]
            pi = tab_ref[d, 1, SUBLANES * m:SUBLANES * (m + 1), :]
            mr, mi = _cmul(pr, pi, sr, si)
            xr = xr + mr
            xi = xi + mi
        shift = SUBLANES - 1 if reverse else 1
        valid = (row < SUBLANES - 1) if reverse else (row >= 1)
        qr = jnp.where(valid, pltpu.roll(xr, shift, 0), 0.0)
        qi = jnp.where(valid, pltpu.roll(xi, shift, 0), 0.0)
        wr = tab_ref[d, 0, 4 * SUBLANES:5 * SUBLANES, :]
        wi = tab_ref[d, 1, 4 * SUBLANES:5 * SUBLANES, :]
        cr, ci = _cmul(wr, wi, er, ei)
        st_ref[rows, c0:c0 + HALF_STATE] = qr + cr
        st_ref[rows, c0 + HALF_STATE:c0 + 2 * HALF_STATE] = qi + ci
        edge = 0 if reverse else SUBLANES - 1
        nr, ni = _cmul(tab_ref[d, 0, 3 * SUBLANES:4 * SUBLANES, :],
                       tab_ref[d, 1, 3 * SUBLANES:4 * SUBLANES, :], er, ei)
        full = (SUBLANES, HALF_STATE)
        return (nr + jnp.broadcast_to(xr[edge:edge + 1], full),
                ni + jnp.broadcast_to(xi[edge:edge + 1], full))

    def body(t, carry):
        fr, fi, br, bi = carry
        fr, fi = tile_scan(t, fr, fi, 0, False)
        br, bi = tile_scan(ntile - 1 - t, br, bi, 1, True)
        return fr, fi, br, bi

    zero = jnp.zeros((SUBLANES, HALF_STATE), F32)
    lax.fori_loop(0, ntile, body, (zero, zero, zero, zero))

    for k in range(nblk):
        rows = pl.ds(k * rb, rb)
        y = jnp.dot(xcat_ref[rows, :], wi_ref[...], preferred_element_type=F32)
        y = y + jnp.dot(st_ref[rows, :].astype(BF16), wq_ref[...], preferred_element_type=F32)
        for p in range(S5_CHUNK):
            o_ref[pl.ds(k * rb * S5_CHUNK + p, rb, stride=S5_CHUNK), :] = y[:, p * LANES:(p + 1) * LANES]


def _s5(z5, wi, wp, wq, tab, rb):
    _, _, nb, seq, _ = z5.shape
    nc = seq // S5_CHUNK
    kdim = S5_CHUNK * LANES
    per_tile = lambda shape: pl.BlockSpec((None,) + shape, lambda o, b: (o,) + (0,) * len(shape))
    return pl.pallas_call(
        functools.partial(_s5_kernel, rb=rb),
        out_shape=jax.ShapeDtypeStruct((N_TILES, nb, seq, LANES), F32),
        grid=(N_TILES, nb),
        in_specs=[pl.BlockSpec((None, None, None, seq, LANES), lambda o, b: (2, o, b, 0, 0)),
                  per_tile((kdim, kdim)), per_tile((kdim, S5_STATE_COLS)),
                  per_tile((S5_STATE_COLS, kdim)), per_tile((2, 2, 5 * SUBLANES, HALF_STATE))],
        out_specs=pl.BlockSpec((None, None, seq, LANES), lambda o, b: (o, b, 0, 0)),
        scratch_shapes=[pltpu.VMEM((nc, kdim), BF16), pltpu.VMEM((nc, S5_STATE_COLS), F32)],
        compiler_params=_params("arbitrary", "arbitrary"),
        name="s5",
    )(z5, wi, wp, wq, tab)


def _s5_weights(lam_re, lam_im, log_step, b_re, b_im, c_re, c_im):
    T = S5_CHUNK
    G, P, H = SSM_GROUPS, SSM_STATE, SSM_GROUP
    dt = jnp.exp(log_step)[:, :, None, None]
    kk = jnp.arange(8 * T + 1, dtype=F32)[None, None, :, None]
    mag = jnp.exp(lam_re[:, :, None, :] * dt * kk)
    ang = lam_im[:, :, None, :] * dt * kk
    pr = mag * jnp.cos(ang)
    pi = mag * jnp.sin(ang)
    ar, ai = pr[:, :, 1], pi[:, :, 1]
    den = lam_re * lam_re + lam_im * lam_im
    nr = ar - 1.0
    cr = (nr * lam_re + ai * lam_im) / den
    ci = (ai * lam_re - nr * lam_im) / den
    bbr = cr[..., None] * b_re - ci[..., None] * b_im
    bbi = cr[..., None] * b_im + ci[..., None] * b_re
    mr = pr[:, :, :T, :, None] * bbr[:, :, None] - pi[:, :, :T, :, None] * bbi[:, :, None]
    mi = pr[:, :, :T, :, None] * bbi[:, :, None] + pi[:, :, :T, :, None] * bbr[:, :, None]
    taps = jnp.einsum('gdon,gdknh->gdkoh', c_re, mr) - jnp.einsum('gdon,gdknh->gdkoh', c_im, mi)
    pos = jnp.arange(T)
    lag = pos[None, :] - pos[:, None]
    kf = taps[:, 0][:, jnp.clip(lag, 0, T - 1)] * (lag >= 0)[None, :, :, None, None]
    kb = taps[:, 1][:, jnp.clip(-lag, 0, T - 1)] * (lag <= 0)[None, :, :, None, None]
    intra = (kf + kb).transpose(0, 1, 4, 2, 3)
    eye = jnp.eye(GROUPS_PER_TILE, dtype=BF16)
    intra = intra.reshape(N_TILES, GROUPS_PER_TILE, T, H, T, H).astype(BF16)
    wi = jnp.einsum('ogphqk,gj->opghqjk', intra, eye).reshape(N_TILES, T * LANES, T * LANES)

    def pmat(m):
        f = m[:, 0, ::-1]
        b = m[:, 1]
        return jnp.stack([f, b], axis=1)
    pm = jnp.stack([pmat(mr), pmat(mi)], axis=2)
    pm = pm.transpose(0, 3, 5, 1, 2, 4).reshape(N_TILES, GROUPS_PER_TILE, T, H, 2, 2, P).astype(BF16)
    wp = jnp.einsum('ogphdcn,gj->opghdcjn', pm, eye).reshape(N_TILES, T * LANES, S5_STATE_COLS)

    lag_f = pos + 1
    lag_b = T - pos
    def qpair(d, lags):
        prd = pr[:, d][:, lags]
        pid = pi[:, d][:, lags]
        on_re = c_re[:, d][:, None] * prd[:, :, None, :] - c_im[:, d][:, None] * pid[:, :, None, :]
        on_im = -(c_re[:, d][:, None] * pid[:, :, None, :] + c_im[:, d][:, None] * prd[:, :, None, :])
        return jnp.stack([on_re, on_im], axis=1)
    qm = jnp.stack([qpair(0, lag_f), qpair(1, lag_b)], axis=1)
    qm = qm.transpose(0, 1, 2, 5, 3, 4).reshape(N_TILES, GROUPS_PER_TILE, 2, 2, P, T, H).astype(BF16)
    wq = jnp.einsum('ogdcnph,gj->odcgnpjh', qm, eye).reshape(N_TILES, S5_STATE_COLS, T * LANES)

    def lanes(x):
        x = x.reshape(N_TILES, GROUPS_PER_TILE, 2, x.shape[2], P)
        return x.transpose(0, 2, 3, 1, 4).reshape(N_TILES, 2, x.shape[3], HALF_STATE)
    steps = jnp.array([T, 2 * T, 4 * T, 8 * T])
    rows_f = T * jnp.arange(SUBLANES)
    rows_b = T * (SUBLANES - 1 - jnp.arange(SUBLANES))
    def table(p):
        head = jnp.repeat(p[:, :, steps], SUBLANES, axis=2)
        tail = jnp.stack([p[:, 0][:, rows_f], p[:, 1][:, rows_b]], axis=1)
        return lanes(jnp.concatenate([head, tail], axis=2))
    tab = jnp.stack([table(pr), table(pi)], axis=2)
    return wi, wp, wq, tab


def _merge_kernel(x_ref, ya_ref, ys_ref, ub_ref, g1_ref, g2_ref, wlo_ref, wga_ref, wgb_ref, wo_ref,
                  d_ref, gf_ref, x1_ref, h2t_ref):
    wide = lambda ref: jnp.concatenate([ref[o] for o in range(N_TILES)], axis=1)
    ya = jnp.dot(wide(ya_ref), wlo_ref[...], preferred_element_type=F32)
    yg = _gelu(wide(ys_ref) + d_ref[...] * wide(ub_ref)).astype(BF16)
    yb = (jnp.dot(yg, wga_ref[...], preferred_element_type=F32)
          * jax.nn.sigmoid(jnp.dot(yg, wgb_ref[...], preferred_element_type=F32)))
    merged = jax.nn.sigmoid(wide(g1_ref)) * ya + jax.nn.sigmoid(wide(g2_ref)) * yb
    x1 = x_ref[...] + jnp.dot(merged.astype(BF16), wo_ref[...], preferred_element_type=F32)
    x1_ref[...] = x1
    h2t_ref[...] = _rms(x1, gf_ref[...]).T.astype(BF16)


def _merge(x2d, ya, ys, z5f, w_lru_out, w_glu_a, w_glu_b, w_out, s5_d, g_ffn, tb):
    n = x2d.shape[0]
    tok = pl.BlockSpec((tb, D_MODEL), lambda i: (i, 0))
    slabs = pl.BlockSpec((N_TILES, tb, LANES), lambda i: (0, i, 0))
    zsl = lambda k: pl.BlockSpec((None, N_TILES, tb, LANES), lambda i: (k, 0, i, 0))
    wsp = pl.BlockSpec((D_MODEL, D_MODEL), lambda i: (0, 0))
    vec = pl.BlockSpec((1, D_MODEL), lambda i: (0, 0))
    return pl.pallas_call(
        _merge_kernel,
        out_shape=(jax.ShapeDtypeStruct((n, D_MODEL), F32), jax.ShapeDtypeStruct((D_MODEL, n), BF16)),
        grid=(n // tb,),
        in_specs=[tok, slabs, slabs, zsl(2), zsl(3), zsl(4), wsp, wsp, wsp, wsp, vec, vec],
        out_specs=(tok, pl.BlockSpec((D_MODEL, tb), lambda i: (0, i))),
        compiler_params=_params("arbitrary"),
        name="merge",
    )(x2d, ya, ys, z5f, z5f, z5f, w_lru_out, w_glu_a, w_glu_b, w_out,
      s5_d.reshape(1, D_MODEL), g_ffn.reshape(1, D_MODEL))


def _top16(s):
    keys = lax.broadcasted_iota(jnp.int32, s.shape, 0)
    slot = lax.broadcasted_iota(jnp.int32, (PEER_TOPK, s.shape[1]), 0)

    def body(k, carry):
        s, rank, vals = carry
        m = jnp.max(s, axis=0, keepdims=True)
        first = jnp.min(jnp.where(s == m, keys, N_KEYS), axis=0, keepdims=True)
        sel = keys == first
        s = jnp.where(sel, -jnp.inf, s)
        rank = jnp.where(sel, k, rank)
        vals = jnp.where(slot == k, m, vals)
        return s, rank, vals

    init = (s, jnp.full(s.shape, PEER_TOPK, jnp.int32), jnp.zeros((PEER_TOPK, s.shape[1]), F32))
    _, rank, vals = lax.fori_loop(0, PEER_TOPK, body, init)
    return vals, rank


def _pair_select(v1, v2):
    w = v1.shape[1]
    r8 = lax.broadcasted_iota(jnp.int32, (SUBLANES, w), 0)
    neg = -jnp.inf
    tiles = []

    big = PEER_TOPK * PEER_TOPK

    def fixed_a(a, b0, limit):
        sums = v1[a:a + 1] + v2[b0:b0 + SUBLANES]
        b = r8 + b0
        ok = b < limit
        tiles.append((jnp.where(ok, sums, neg), jnp.where(ok, a * PEER_TOPK + b, big), a, None))

    def fixed_b(b, a0, lo, hi):
        sums = v1[a0:a0 + SUBLANES] + v2[b:b + 1]
        a = r8 + a0
        ok = jnp.where(a >= lo, a, hi) < hi
        tiles.append((jnp.where(ok, sums, neg), jnp.where(ok, a * PEER_TOPK + b, big), None, a0))

    fixed_a(0, 0, 16); fixed_a(0, 8, 16); fixed_a(1, 0, 8); fixed_a(2, 0, 5); fixed_a(3, 0, 4)
    fixed_b(0, 0, 4, 16); fixed_b(0, 8, 4, 16); fixed_b(1, 0, 4, 8); fixed_b(2, 0, 4, 5)

    top = v1[0:1] + v2[0:1]
    sums = [t[0] for t in tiles]
    picked = [jnp.zeros((SUBLANES, w), F32) for _ in tiles]
    for _ in range(PEER_TOPK):
        m = functools.reduce(jnp.maximum, sums)
        m = jnp.max(m, axis=0, keepdims=True)
        cand = [jnp.where(s == m, t[1], big) for s, t in zip(sums, tiles)]
        first = jnp.min(functools.reduce(jnp.minimum, cand), axis=0, keepdims=True)
        for n, t in enumerate(tiles):
            sel = t[1] == first
            picked[n] = jnp.where(sel, 1.0, picked[n])
            sums[n] = jnp.where(sel, neg, sums[n])

    r16 = lax.broadcasted_iota(jnp.int32, (PEER_TOPK, w), 0)
    cnt = jnp.zeros((PEER_TOPK, w), F32)
    den = jnp.zeros((SUBLANES, w), F32)
    lo_rows = jnp.zeros((SUBLANES, w), F32)
    hi_rows = jnp.zeros((SUBLANES, w), F32)
    for n, t in enumerate(tiles):
        den = den + picked[n] * jnp.exp(jnp.where(picked[n] > 0, t[0], top) - top)
        if t[2] is not None:
            cnt = cnt + jnp.where(r16 == t[2], jnp.sum(picked[n], axis=0, keepdims=True), 0.0)
        elif t[3] == 0:
            lo_rows = lo_rows + picked[n]
        else:
            hi_rows = hi_rows + picked[n]
    cnt = cnt + jnp.concatenate([lo_rows, hi_rows], axis=0)
    return cnt, jnp.sum(den, axis=0, keepdims=True)


def _sort_pairs(n):
    pairs = []

    def merge(lo, m, r):
        step = 2 * r
        if step < m:
            merge(lo, m, step)
            merge(lo + r, m, step)
            pairs.extend((i, i + r) for i in range(lo + r, lo + m - r, step))
        else:
            pairs.append((lo, lo + r))

    def sort(lo, m):
        if m > 1:
            sort(lo, m // 2)
            sort(lo + m // 2, m // 2)
            merge(lo, m, 1)

    sort(0, n)
    return pairs


_SORT16 = _sort_pairs(PEER_TOPK)


def _exchange(v, i, j):
    a, b = v[i], v[j]
    if b is None:
        return
    if a is None:
        v[i], v[j] = b, None
        return
    v[i], v[j] = jnp.maximum(a, b), jnp.minimum(a, b)


def _sorted_top16(tiles):
    v = list(tiles) + [None] * (PEER_TOPK - len(tiles))
    for i, j in _SORT16:
        _exchange(v, i, j)
    for s in (1, 2, 4):
        r = [None if x is None else pltpu.roll(x, s, 0) for x in v]
        merged = []
        for k in range(PEER_TOPK):
            a, b = v[k], r[PEER_TOPK - 1 - k]
            merged.append(b if a is None else a if b is None else jnp.maximum(a, b))
        v = merged
        for d in (8, 4, 2, 1):
            for i in range(PEER_TOPK):
                if not i & d:
                    _exchange(v, i, i + d)
    return v


def _dup_words(x):
    bits = pltpu.bitcast(x.astype(BF16).astype(F32), jnp.uint32)
    return bits | (bits >> 16)


def _rows_bf16(words, rows):
    return pltpu.bitcast(jnp.broadcast_to(words, (rows // 2, words.shape[1])), BF16)


def _count_leading(test, rows):
    h8 = test(rows[7])
    h4 = test(jnp.where(h8, rows[11], rows[3]))
    h2 = test(jnp.where(h8, jnp.where(h4, rows[13], rows[9]), jnp.where(h4, rows[5], rows[1])))
    lo = jnp.where(h4, jnp.where(h2, rows[6], rows[4]), jnp.where(h2, rows[2], rows[0]))
    hi = jnp.where(h4, jnp.where(h2, rows[14], rows[12]), jnp.where(h2, rows[10], rows[8]))
    h1 = test(jnp.where(h8, hi, lo))
    n = ((jnp.where(h8, 8.0, 0.0) + jnp.where(h4, 4.0, 0.0))
         + (jnp.where(h2, 2.0, 0.0) + jnp.where(h1, 1.0, 0.0)))
    return jnp.where(test(rows[PEER_TOPK - 1]), float(PEER_TOPK), n)


def _route_fast(a1, a2):
    w = a1.shape[1]
    split = lambda a: [a[SUBLANES * k:SUBLANES * (k + 1)] for k in range(N_KEYS // SUBLANES)]
    t1 = _sorted_top16(split(a1))
    t2 = _sorted_top16(split(a2))
    row = lax.broadcasted_iota(jnp.int32, (SUBLANES, w), 0)

    def column(t, base):
        out = t[base]
        for r in range(1, SUBLANES):
            out = jnp.where(row == r, t[base + r], out)
        return out

    v1 = [column(t1, 0), column(t1, SUBLANES)]
    v2 = [column(t2, 0), column(t2, SUBLANES)]
    neg = -jnp.inf
    cands = [t1[0] + v2[0], t1[0] + v2[1], t1[1] + v2[0],
             jnp.where(row < 5, t1[2] + v2[0], neg), jnp.where(row < 4, t1[3] + v2[0], neg),
             jnp.where(row >= 4, v1[0] + t2[0], neg), v1[1] + t2[0],
             jnp.where(row >= 4, v1[0] + t2[1], neg), jnp.where(row == 4, v1[0] + t2[2], neg)]
    theta = _sorted_top16(cands)[PEER_TOPK - 1][0:1]
    top = t1[0] + t2[0]
    n_pairs = jnp.zeros((SUBLANES, w), F32)
    den = jnp.zeros((SUBLANES, w), F32)
    for c in cands:
        hit = c >= theta
        n_pairs = n_pairs + jnp.where(hit, 1.0, 0.0)
        den = den + jnp.where(hit, jnp.exp(jnp.where(hit, c, top) - top), 0.0)
    n_pairs = jnp.sum(n_pairs, axis=0, keepdims=True)
    inv_den = 1.0 / jnp.sum(den, axis=0, keepdims=True)

    tall = lambda t: pltpu.repeat(t, N_KEYS // SUBLANES, axis=0)
    in1 = a1 >= tall(t1[PEER_TOPK - 1])
    in2 = a2 >= tall(t2[PEER_TOPK - 1])
    n1 = jnp.sum(jnp.where(in1, 1.0, 0.0), axis=0, keepdims=True)
    n2 = jnp.sum(jnp.where(in2, 1.0, 0.0), axis=0, keepdims=True)
    theta_t = jnp.broadcast_to(theta, a1.shape)
    t2_t = [tall(t) for t in t2]
    ck = _count_leading(lambda v: a1 + v >= theta_t, t2_t)
    r2 = _count_leading(lambda v: v > a2, t2_t)
    ck = jnp.where(in1, ck, 0.0)
    pk = jnp.where(in1, jnp.exp(jnp.where(in1, a1, 0.0) - tall(t1[0])), 0.0)
    qk = jnp.where(in2, jnp.exp(jnp.where(in2, a2, 0.0) - t2_t[0]) * inv_den, 0.0)
    k = float(PEER_TOPK)
    flag = jnp.where((n1 != k) | (n2 != k) | (n_pairs != k), 1.0, 0.0)
    return pk, ck, qk, r2, flag


def _route_exact(a1, a2):
    v1, rank1 = _top16(a1)
    v2, rank2 = _top16(a2)
    cnt, den = _pair_select(v1, v2)
    in1 = rank1 < PEER_TOPK
    in2 = rank2 < PEER_TOPK
    ck = jnp.zeros(a1.shape, F32)
    for a in range(PEER_TOPK):
        ck = jnp.where(rank1 == a, cnt[a:a + 1], ck)
    pk = jnp.where(in1, jnp.exp(jnp.where(in1, a1, v1[0:1]) - v1[0:1]), 0.0)
    qk = jnp.where(in2, jnp.exp(jnp.where(in2, a2, v2[0:1]) - v2[0:1]) / den, 0.0)
    return pk, ck, qk, rank2.astype(F32)


def _route_kernel(h2t_ref, wqt_ref, keys_ref, p_ref, c_ref, q_ref, r2_ref, s_ref):
    qt = jnp.dot(wqt_ref[...], h2t_ref[...], preferred_element_type=F32)
    s_ref[0] = jnp.dot(keys_ref[0], qt[:PEER_HALF].astype(BF16), preferred_element_type=F32)
    s_ref[1] = jnp.dot(keys_ref[1], qt[PEER_HALF:].astype(BF16), preferred_element_type=F32)
    ntile = s_ref.shape[2] // LANES

    def store(cols, pk, ck, qk, r2):
        p_ref[:, cols] = _dup_words(pk)
        c_ref[:, cols] = _dup_words(ck)
        q_ref[:, cols] = qk.astype(q_ref.dtype)
        r2_ref[:, cols] = r2.astype(r2_ref.dtype)

    flag = jnp.zeros((1, LANES), F32)
    for lt in range(ntile):
        cols = slice(lt * LANES, (lt + 1) * LANES)
        pk, ck, qk, r2, f = _route_fast(s_ref[0, :, cols], s_ref[1, :, cols])
        store(cols, pk, ck, qk, r2)
        flag = jnp.maximum(flag, f)

    @pl.when(jnp.max(flag) > 0.0)
    def _():
        for lt in range(ntile):
            cols = slice(lt * LANES, (lt + 1) * LANES)
            store(cols, *_route_exact(s_ref[0, :, cols], s_ref[1, :, cols]))


def _route(h2t, wqt, keys, tb):
    n = h2t.shape[1]
    words = jax.ShapeDtypeStruct((PEER_HEADS, N_KEYS, n), jnp.uint32)
    halfs = jax.ShapeDtypeStruct((PEER_HEADS, N_KEYS, n), BF16)
    osp = pl.BlockSpec((None, N_KEYS, tb), lambda i, h: (h, 0, i))
    return pl.pallas_call(
        _route_kernel,
        out_shape=(words, words, halfs, halfs),
        grid=(n // tb, PEER_HEADS),
        in_specs=[pl.BlockSpec((D_MODEL, tb), lambda i, h: (0, i)),
                  pl.BlockSpec((2 * PEER_HALF, D_MODEL), lambda i, h: (h, 0)),
                  pl.BlockSpec((None, 2, N_KEYS, PEER_HALF), lambda i, h: (h, 0, 0, 0))],
        out_specs=(osp, osp, osp, osp),
        scratch_shapes=[pltpu.VMEM((2, N_KEYS, tb), F32)],
        compiler_params=_params("arbitrary", "arbitrary"),
        name="route",
    )(h2t, wqt, keys)


def _peer_kernel(h2t_ref, u_ref, vt_ref, p_ref, c_ref, q_ref, r2_ref, x1_ref, g_ref, y_ref, acc_ref,
                 act_a, act_b, *, rows_per_step):
    e = pl.program_id(1)
    last = pl.num_programs(1) - 1

    def score(dst_ref):
        dst_ref[...] = jnp.dot(u_ref[...], h2t_ref[...], preferred_element_type=F32)

    def combine(src_ref):
        parts = []
        for ii in range(rows_per_step):
            gate = None
            for h in range(PEER_HEADS):
                q = q_ref[h]
                hit = r2_ref[h] < _rows_bf16(c_ref[h, ii:ii + 1, :], N_KEYS)
                term = jnp.where(hit, q, jnp.zeros_like(q)) * _rows_bf16(p_ref[h, ii:ii + 1, :], N_KEYS)
                gate = term if gate is None else gate + term
            a = src_ref[ii * N_KEYS:(ii + 1) * N_KEYS, :].astype(BF16)
            parts.append(gate * _gelu(a))
        wt = jnp.concatenate(parts, axis=0)
        acc_ref[...] += jnp.dot(vt_ref[...], wt, preferred_element_type=F32)

    @pl.when(e == 0)
    def _():
        acc_ref[...] = jnp.zeros_like(acc_ref)
        score(act_a)

    @pl.when((e % 2 == 1) & (e < last))
    def _():
        score(act_b)
        combine(act_a)

    @pl.when((e % 2 == 0) & (e > 0) & (e < last))
    def _():
        score(act_a)
        combine(act_b)

    @pl.when(e == last)
    def _():
        combine(act_b if (N_KEYS // rows_per_step) % 2 == 0 else act_a)
        x2 = x1_ref[...] + acc_ref[...].T
        y_ref[...] = _rms(x2, g_ref[...])


def _peer(h2t, u, vt, pk, ck, qk, r2k, x1, g_final, tb, rows_per_step):
    n = h2t.shape[1]
    eb = rows_per_step * N_KEYS
    nblk = N_KEYS // rows_per_step
    scored = lambda e: jnp.minimum(e, nblk - 1)
    combined = lambda e: jnp.maximum(e - 1, 0)
    small = pl.BlockSpec((PEER_HEADS, rows_per_step, tb), lambda i, e: (0, combined(e), i))
    full = pl.BlockSpec((PEER_HEADS, N_KEYS, tb), lambda i, e: (0, 0, i))
    tok = pl.BlockSpec((tb, D_MODEL), lambda i, e: (i, 0))
    return pl.pallas_call(
        functools.partial(_peer_kernel, rows_per_step=rows_per_step),
        out_shape=jax.ShapeDtypeStruct((n, D_MODEL), F32),
        grid=(n // tb, nblk + 1),
        in_specs=[pl.BlockSpec((D_MODEL, tb), lambda i, e: (0, i)),
                  pl.BlockSpec((eb, D_MODEL), lambda i, e: (scored(e), 0)),
                  pl.BlockSpec((D_MODEL, eb), lambda i, e: (0, combined(e))),
                  small, small, full, full, tok,
                  pl.BlockSpec((1, D_MODEL), lambda i, e: (0, 0))],
        out_specs=tok,
        scratch_shapes=[pltpu.VMEM((D_MODEL, tb), F32), pltpu.VMEM((eb, tb), F32),
                        pltpu.VMEM((eb, tb), F32)],
        compiler_params=_params("arbitrary", "arbitrary"),
        name="peer",
    )(h2t, u, vt, pk, ck, qk, r2k, x1, g_final.reshape(1, D_MODEL))


def _layer(x, prm):
    nb, seq, _ = x.shape
    n = nb * seq
    x2d = x.reshape(n, D_MODEL)
    z5 = _in_proj(x2d, prm['g_mix'], prm['w_in'], 256)
    z5s = z5.reshape(z5.shape[0], N_TILES, nb, seq, LANES)
    ya = _lru(z5s, prm['conv_w'], prm['conv_b'], *prm['lru'], rt=256)
    ys = _s5(z5s, *prm['s5'], rb=min(256, seq // S5_CHUNK))
    x1, h2t = _merge(x2d, ya.reshape(N_TILES, n, LANES), ys.reshape(N_TILES, n, LANES), z5,
                     prm['w_lru_out'], prm['w_glu_a'], prm['w_glu_b'], prm['w_out'],
                     prm['s5_d'], prm['g_ffn'], 256)
    pk, ck, qk, r2k = _route(h2t, prm['wqt'], prm['keys'], min(1024, n))
    y = _peer(h2t, prm['u'], prm['vt'], pk, ck, qk, r2k, x1, prm['g_final'], 256, 32)
    return y.reshape(nb, seq, D_MODEL)


def kernel(x_prompt, x_sample, g_mix, w_in, conv_w, conv_b, lru_wr, lru_br, lru_wi, lru_bi, lru_lam,
           w_lru_out, s5_lam_re, s5_lam_im, s5_log_step, s5_b_re, s5_b_im, s5_c_re, s5_c_im, s5_d,
           w_glu_a, w_glu_b, w_out, g_ffn, w_query, sub_keys, expert_u, expert_v, g_final):
    depth = g_mix.shape[0]
    xp, xs = x_prompt, x_sample
    for l in range(depth):
        prm = {
            'g_mix': g_mix[l], 'w_in': w_in[l].astype(BF16),
            'conv_w': conv_w[l], 'conv_b': conv_b[l].reshape(1, D_MODEL),
            'lru': _lru_weights(lru_wr[l], lru_br[l], lru_wi[l], lru_bi[l], lru_lam[l]),
            's5': _s5_weights(s5_lam_re[l], s5_lam_im[l], s5_log_step[l], s5_b_re[l], s5_b_im[l],
                              s5_c_re[l], s5_c_im[l]),
            'w_lru_out': w_lru_out[l].astype(BF16), 'w_glu_a': w_glu_a[l].astype(BF16),
            'w_glu_b': w_glu_b[l].astype(BF16), 'w_out': w_out[l].astype(BF16),
            's5_d': s5_d[l], 'g_ffn': g_ffn[l],
            'wqt': w_query[l].astype(BF16).T,
            'keys': sub_keys[l].astype(BF16),
            'u': expert_u[l].astype(BF16), 'vt': expert_v[l].astype(BF16).T,
            'g_final': g_final,
        }
        assert depth == 1
        xp = _layer(xp, prm)
        xs = _layer(xs, prm)
    return (xp, xs)
```

```python
import functools
import math

import jax
import jax.numpy as jnp
from jax import lax
from jax.experimental import pallas as pl
from jax.experimental.pallas import tpu as pltpu

F32 = jnp.float32
BF16 = jnp.bfloat16

LANES = 128
SUBLANES = 8
VMEM_LIMIT_BYTES = 56 * 1024 * 1024

D_MODEL = 1024
LRU_HEADS = 16
LRU_HEAD_DIM = D_MODEL // LRU_HEADS
LRU_C = 8.0
SSM_GROUP = 16
SSM_GROUPS = D_MODEL // SSM_GROUP
SSM_STATE = 64
N_KEYS = 128
PEER_HEADS = 8
PEER_TOPK = 16
PEER_HALF = 128
EPS = 1e-6

N_TILES = D_MODEL // LANES
GROUPS_PER_TILE = LANES // SSM_GROUP
S5_CHUNK = 8
S5_STATE_COLS = 2 * 2 * GROUPS_PER_TILE * SSM_STATE
HALF_STATE = GROUPS_PER_TILE * SSM_STATE


def _gelu(x):
    c = math.sqrt(2.0 / math.pi)
    return 0.5 * x * (1.0 + jnp.tanh(c * (x + 0.044715 * (x * x * x))))


def _rms(x, g):
    ms = jnp.mean(x * x, axis=-1, keepdims=True)
    return x * lax.rsqrt(ms + EPS) * g


def _params(*sem, flags=None):
    return pltpu.CompilerParams(dimension_semantics=sem, vmem_limit_bytes=VMEM_LIMIT_BYTES, flags=flags)


def _in_proj_kernel(x_ref, g_ref, w_ref, z_ref):
    h = _rms(x_ref[...], g_ref[...]).astype(BF16)
    for j in range(z_ref.shape[0]):
        r = jnp.dot(h, w_ref[:, j * D_MODEL:(j + 1) * D_MODEL], preferred_element_type=F32)
        for o in range(N_TILES):
            z_ref[j, o] = r[:, o * LANES:(o + 1) * LANES]


def _in_proj(x2d, g_mix, w_in, tb):
    n = x2d.shape[0]
    nj = w_in.shape[1] // D_MODEL
    return pl.pallas_call(
        _in_proj_kernel,
        out_shape=jax.ShapeDtypeStruct((nj, N_TILES, n, LANES), F32),
        grid=(n // tb,),
        in_specs=[pl.BlockSpec((tb, D_MODEL), lambda i: (i, 0)),
                  pl.BlockSpec((1, D_MODEL), lambda i: (0, 0)),
                  pl.BlockSpec((D_MODEL, nj * D_MODEL), lambda i: (0, 0))],
        out_specs=pl.BlockSpec((nj, N_TILES, tb, LANES), lambda i: (0, 0, i, 0)),
        compiler_params=_params("arbitrary"),
        name="in_proj",
    )(x2d, g_mix.reshape(1, D_MODEL), w_in)


def _group_scan(a, b, reverse):
    row = lax.broadcasted_iota(jnp.int32, a.shape, 0) & (SUBLANES - 1)
    n = a.shape[0]
    for s in (1, 2, 4):
        if reverse:
            a_s = pltpu.roll(a, n - s, 0)
            b_s = pltpu.roll(b, n - s, 0)
            valid = row < SUBLANES - s
        else:
            a_s = pltpu.roll(a, s, 0)
            b_s = pltpu.roll(b, s, 0)
            valid = row >= s
        b = a * jnp.where(valid, b_s, 0.0) + b
        a = a * jnp.where(valid, a_s, 1.0)
    return a, b


def _carry_groups(a, b, h, reverse):
    groups = a.shape[0] // SUBLANES
    outs = [None] * groups
    order = range(groups - 1, -1, -1) if reverse else range(groups)
    edge = 0 if reverse else SUBLANES - 1
    for k in order:
        ak = a[k * SUBLANES:(k + 1) * SUBLANES]
        bk = b[k * SUBLANES:(k + 1) * SUBLANES]
        outs[k] = ak * h + bk
        h = ak[edge:edge + 1] * h + bk[edge:edge + 1]
    return jnp.concatenate(outs, axis=0), h


def _lru_kernel(xa_ref, ga_ref, cw_ref, cb_ref, wgf_ref, wgb_ref, bgf_ref, bgb_ref, sp_ref,
                o_ref, hf_ref, hb_ref, xc_ref, *, rt):
    seq = xa_ref.shape[0]
    nt = seq // rt
    cw = cw_ref[...]
    cb = cb_ref[...]

    def conv_tile(j, c):
        r0 = pl.multiple_of(j * rt, rt)
        cur = xa_ref[pl.ds(r0, rt), :]
        p0 = pl.multiple_of(jnp.maximum(r0 - SUBLANES, 0), SUBLANES)
        n0 = pl.multiple_of(jnp.minimum(r0 + rt, seq - SUBLANES), SUBLANES)
        prev = jnp.where(j > 0, xa_ref[pl.ds(p0, SUBLANES), :], 0.0)
        nxt = jnp.where(j < nt - 1, xa_ref[pl.ds(n0, SUBLANES), :], 0.0)
        ext = jnp.concatenate([prev, cur, nxt], axis=0)
        n = rt + 2 * SUBLANES
        xm2 = pltpu.roll(ext, 2, 0)[SUBLANES:SUBLANES + rt]
        xm1 = pltpu.roll(ext, 1, 0)[SUBLANES:SUBLANES + rt]
        xp1 = pltpu.roll(ext, n - 1, 0)[SUBLANES:SUBLANES + rt]
        xc_ref[pl.ds(r0, rt), :] = cw[0:1] * xm2 + cw[1:2] * xm1 + cw[2:3] * cur + cw[3:4] * xp1 + cb
        return c

    lax.fori_loop(0, nt, conv_tile, 0)

    def sigmoid(x):
        return 0.5 * jnp.tanh(0.5 * x) + 0.5

    def direction(j, h, wg_ref, bg_ref, sp, dst_ref, reverse):
        rows = pl.ds(pl.multiple_of(j * rt, rt), rt)
        xc = xc_ref[rows, :]
        gz = jnp.dot(xc.astype(BF16), wg_ref[...], preferred_element_type=F32) + bg_ref[...]
        r = sigmoid(gz[:, :LANES])
        i = sigmoid(gz[:, LANES:])
        log_a = -LRU_C * r * sp
        a = jnp.exp(log_a)
        v = -jnp.tanh(log_a) * (1.0 + a * a)
        b = jnp.where(v > 0.0, v * lax.rsqrt(v), 0.0) * i * xc
        a_g, b_g = _group_scan(a, b, reverse)
        h_tile, h = _carry_groups(a_g, b_g, h, reverse)
        dst_ref[rows, :] = h_tile
        return h

    def body(j, carry):
        hf, hb = carry
        hf = direction(j, hf, wgf_ref, bgf_ref, sp_ref[0:1, :], hf_ref, False)
        hb = direction(nt - 1 - j, hb, wgb_ref, bgb_ref, sp_ref[1:2, :], hb_ref, True)
        return hf, hb

    zero = jnp.zeros((1, LANES), F32)
    lax.fori_loop(0, nt, body, (zero, zero), unroll=2)

    def finish(j, c):
        rows = pl.ds(pl.multiple_of(j * rt, rt), rt)
        y = (hf_ref[rows, :] + hb_ref[rows, :]) * _gelu(ga_ref[rows, :])
        o_ref[rows, :] = y.astype(o_ref.dtype)
        return c

    lax.fori_loop(0, nt, finish, 0)


def _lru(z5, conv_w, conv_b, wgf, wgb, bgf, bgb, sp, rt):
    _, _, nb, seq, _ = z5.shape
    slab = lambda k: pl.BlockSpec((None, None, None, seq, LANES), lambda b, o: (k, o, b, 0, 0))
    per_tile = lambda shape: pl.BlockSpec((None,) + shape, lambda b, o: (o,) + (0,) * len(shape))
    return pl.pallas_call(
        functools.partial(_lru_kernel, rt=rt),
        out_shape=jax.ShapeDtypeStruct((N_TILES, nb, seq, LANES), BF16),
        grid=(nb, N_TILES),
        in_specs=[slab(0), slab(1),
                  pl.BlockSpec((4, LANES), lambda b, o: (0, o)),
                  pl.BlockSpec((1, LANES), lambda b, o: (0, o)),
                  per_tile((LANES, 2 * LANES)), per_tile((LANES, 2 * LANES)),
                  per_tile((1, 2 * LANES)), per_tile((1, 2 * LANES)),
                  pl.BlockSpec((2, LANES), lambda b, o: (0, o))],
        out_specs=pl.BlockSpec((None, None, seq, LANES), lambda b, o: (o, b, 0, 0)),
        scratch_shapes=[pltpu.VMEM((seq, LANES), F32)] * 3,
        compiler_params=_params("arbitrary", "arbitrary"),
        name="lru",
    )(z5, z5, conv_w, conv_b, wgf, wgb, bgf, bgb, sp)


def _lru_weights(lru_wr, lru_br, lru_wi, lru_bi, lru_lam):
    eye = jnp.eye(2, dtype=F32)

    def blockdiag(w):
        w = w.reshape(N_TILES, 2, LRU_HEAD_DIM, LRU_HEAD_DIM)
        return jnp.einsum('ohij,hk->ohikj', w, eye).reshape(N_TILES, LANES, LANES)

    def direction(d):
        wg = jnp.concatenate([blockdiag(lru_wr[d]), blockdiag(lru_wi[d])], axis=-1).astype(BF16)
        bg = jnp.concatenate([lru_br[d].reshape(N_TILES, 1, LANES),
                              lru_bi[d].reshape(N_TILES, 1, LANES)], axis=-1)
        return wg, bg

    wgf, bgf = direction(0)
    wgb, bgb = direction(1)
    return wgf, wgb, bgf, bgb, jax.nn.softplus(-lru_lam)


def _cmul(ar, ai, br, bi):
    return ar * br - ai * bi, ar * bi + ai * br


def _s5_kernel(u_ref, wi_ref, wp_ref, wq_ref, tab_ref, o_ref, xcat_ref, st_ref, *, rb):
    seq = u_ref.shape[0]
    nc = seq // S5_CHUNK
    nblk = nc // rb
    ntile = nc // SUBLANES

    for p in range(S5_CHUNK):
        xcat_ref[:, p * LANES:(p + 1) * LANES] = u_ref[pl.ds(p, nc, stride=S5_CHUNK), :].astype(BF16)

    for k in range(nblk):
        rows = pl.ds(k * rb, rb)
        st_ref[rows, :] = jnp.dot(xcat_ref[rows, :], wp_ref[...], preferred_element_type=F32)

    row = lax.broadcasted_iota(jnp.int32, (SUBLANES, HALF_STATE), 0)

    def tile_scan(t, er, ei, d, reverse):
        rows = pl.ds(pl.multiple_of(t * SUBLANES, SUBLANES), SUBLANES)
        c0 = d * 2 * HALF_STATE
        xr = st_ref[rows, c0:c0 + HALF_STATE]
        xi = st_ref[rows, c0 + HALF_STATE:c0 + 2 * HALF_STATE]
        for m, s in enumerate((1, 2, 4)):
            shift = SUBLANES - s if reverse else s
            valid = (row < SUBLANES - s) if reverse else (row >= s)
            sr = jnp.where(valid, pltpu.roll(xr, shift, 0), 0.0)
            si = jnp.where(valid, pltpu.roll(xi, shift, 0), 0.0)
            pr = tab_ref[d, 0, SUBLANES * m:SUBLANES * (m + 1), :]
            pi = tab_ref[d, 1, SUBLANES * m:SUBLANES * (m + 1), :]
            mr, mi = _cmul(pr, pi, sr, si)
            xr = xr + mr
            xi = xi + mi
        shift = SUBLANES - 1 if reverse else 1
        valid = (row < SUBLANES - 1) if reverse else (row >= 1)
        qr = jnp.where(valid, pltpu.roll(xr, shift, 0), 0.0)
        qi = jnp.where(valid, pltpu.roll(xi, shift, 0), 0.0)
        wr = tab_ref[d, 0, 4 * SUBLANES:5 * SUBLANES, :]
        wi = tab_ref[d, 1, 4 * SUBLANES:5 * SUBLANES, :]
        cr, ci = _cmul(wr, wi, er, ei)
        st_ref[rows, c0:c0 + HALF_STATE] = qr + cr
        st_ref[rows, c0 + HALF_STATE:c0 + 2 * HALF_STATE] = qi + ci
        edge = 0 if reverse else SUBLANES - 1
        nr, ni = _cmul(tab_ref[d, 0, 3 * SUBLANES:4 * SUBLANES, :],
                       tab_ref[d, 1, 3 * SUBLANES:4 * SUBLANES, :], er, ei)
        full = (SUBLANES, HALF_STATE)
        return (nr + jnp.broadcast_to(xr[edge:edge + 1], full),
                ni + jnp.broadcast_to(xi[edge:edge + 1], full))

    def body(t, carry):
        fr, fi, br, bi = carry
        fr, fi = tile_scan(t, fr, fi, 0, False)
        br, bi = tile_scan(ntile - 1 - t, br, bi, 1, True)
        return fr, fi, br, bi

    zero = jnp.zeros((SUBLANES, HALF_STATE), F32)
    lax.fori_loop(0, ntile, body, (zero, zero, zero, zero))

    for k in range(nblk):
        rows = pl.ds(k * rb, rb)
        y = jnp.dot(xcat_ref[rows, :], wi_ref[...], preferred_element_type=F32)
        y = y + jnp.dot(st_ref[rows, :].astype(BF16), wq_ref[...], preferred_element_type=F32)
        for p in range(S5_CHUNK):
            o_ref[pl.ds(k * rb * S5_CHUNK + p, rb, stride=S5_CHUNK), :] = y[:, p * LANES:(p + 1) * LANES]


def _s5(z5, wi, wp, wq, tab, rb):
    _, _, nb, seq, _ = z5.shape
    nc = seq // S5_CHUNK
    kdim = S5_CHUNK * LANES
    per_tile = lambda shape: pl.BlockSpec((None,) + shape, lambda o, b: (o,) + (0,) * len(shape))
    return pl.pallas_call(
        functools.partial(_s5_kernel, rb=rb),
        out_shape=jax.ShapeDtypeStruct((N_TILES, nb, seq, LANES), F32),
        grid=(N_TILES, nb),
        in_specs=[pl.BlockSpec((None, None, None, seq, LANES), lambda o, b: (2, o, b, 0, 0)),
                  per_tile((kdim, kdim)), per_tile((kdim, S5_STATE_COLS)),
                  per_tile((S5_STATE_COLS, kdim)), per_tile((2, 2, 5 * SUBLANES, HALF_STATE))],
        out_specs=pl.BlockSpec((None, None, seq, LANES), lambda o, b: (o, b, 0, 0)),
        scratch_shapes=[pltpu.VMEM((nc, kdim), BF16), pltpu.VMEM((nc, S5_STATE_COLS), F32)],
        compiler_params=_params("arbitrary", "arbitrary"),
        name="s5",
    )(z5, wi, wp, wq, tab)


def _s5_weights(lam_re, lam_im, log_step, b_re, b_im, c_re, c_im):
    T = S5_CHUNK
    G, P, H = SSM_GROUPS, SSM_STATE, SSM_GROUP
    dt = jnp.exp(log_step)[:, :, None, None]
    kk = jnp.arange(8 * T + 1, dtype=F32)[None, None, :, None]
    mag = jnp.exp(lam_re[:, :, None, :] * dt * kk)
    ang = lam_im[:, :, None, :] * dt * kk
    pr = mag * jnp.cos(ang)
    pi = mag * jnp.sin(ang)
    ar, ai = pr[:, :, 1], pi[:, :, 1]
    den = lam_re * lam_re + lam_im * lam_im
    nr = ar - 1.0
    cr = (nr * lam_re + ai * lam_im) / den
    ci = (ai * lam_re - nr * lam_im) / den
    bbr = cr[..., None] * b_re - ci[..., None] * b_im
    bbi = cr[..., None] * b_im + ci[..., None] * b_re
    mr = pr[:, :, :T, :, None] * bbr[:, :, None] - pi[:, :, :T, :, None] * bbi[:, :, None]
    mi = pr[:, :, :T, :, None] * bbi[:, :, None] + pi[:, :, :T, :, None] * bbr[:, :, None]
    taps = jnp.einsum('gdon,gdknh->gdkoh', c_re, mr) - jnp.einsum('gdon,gdknh->gdkoh', c_im, mi)
    pos = jnp.arange(T)
    lag = pos[None, :] - pos[:, None]
    kf = taps[:, 0][:, jnp.clip(lag, 0, T - 1)] * (lag >= 0)[None, :, :, None, None]
    kb = taps[:, 1][:, jnp.clip(-lag, 0, T - 1)] * (lag <= 0)[None, :, :, None, None]
    intra = (kf + kb).transpose(0, 1, 4, 2, 3)
    eye = jnp.eye(GROUPS_PER_TILE, dtype=BF16)
    intra = intra.reshape(N_TILES, GROUPS_PER_TILE, T, H, T, H).astype(BF16)
    wi = jnp.einsum('ogphqk,gj->opghqjk', intra, eye).reshape(N_TILES, T * LANES, T * LANES)

    def pmat(m):
        f = m[:, 0, ::-1]
        b = m[:, 1]
        return jnp.stack([f, b], axis=1)
    pm = jnp.stack([pmat(mr), pmat(mi)], axis=2)
    pm = pm.transpose(0, 3, 5, 1, 2, 4).reshape(N_TILES, GROUPS_PER_TILE, T, H, 2, 2, P).astype(BF16)
    wp = jnp.einsum('ogphdcn,gj->opghdcjn', pm, eye).reshape(N_TILES, T * LANES, S5_STATE_COLS)

    lag_f = pos + 1
    lag_b = T - pos
    def qpair(d, lags):
        prd = pr[:, d][:, lags]
        pid = pi[:, d][:, lags]
        on_re = c_re[:, d][:, None] * prd[:, :, None, :] - c_im[:, d][:, None] * pid[:, :, None, :]
        on_im = -(c_re[:, d][:, None] * pid[:, :, None, :] + c_im[:, d][:, None] * prd[:, :, None, :])
        return jnp.stack([on_re, on_im], axis=1)
    qm = jnp.stack([qpair(0, lag_f), qpair(1, lag_b)], axis=1)
    qm = qm.transpose(0, 1, 2, 5, 3, 4).reshape(N_TILES, GROUPS_PER_TILE, 2, 2, P, T, H).astype(BF16)
    wq = jnp.einsum('ogdcnph,gj->odcgnpjh', qm, eye).reshape(N_TILES, S5_STATE_COLS, T * LANES)

    def lanes(x):
        x = x.reshape(N_TILES, GROUPS_PER_TILE, 2, x.shape[2], P)
        return x.transpose(0, 2, 3, 1, 4).reshape(N_TILES, 2, x.shape[3], HALF_STATE)
    steps = jnp.array([T, 2 * T, 4 * T, 8 * T])
    rows_f = T * jnp.arange(SUBLANES)
    rows_b = T * (SUBLANES - 1 - jnp.arange(SUBLANES))
    def table(p):
        head = jnp.repeat(p[:, :, steps], SUBLANES, axis=2)
        tail = jnp.stack([p[:, 0][:, rows_f], p[:, 1][:, rows_b]], axis=1)
        return lanes(jnp.concatenate([head, tail], axis=2))
    tab = jnp.stack([table(pr), table(pi)], axis=2)
    return wi, wp, wq, tab


def _merge_kernel(x_ref, ya_ref, ys_ref, ub_ref, g1_ref, g2_ref, wlo_ref, wga_ref, wgb_ref, wo_ref,
                  d_ref, gf_ref, x1_ref, h2t_ref):
    wide = lambda ref: jnp.concatenate([ref[o] for o in range(N_TILES)], axis=1)
    ya = jnp.dot(wide(ya_ref), wlo_ref[...], preferred_element_type=F32)
    yg = _gelu(wide(ys_ref) + d_ref[...] * wide(ub_ref)).astype(BF16)
    yb = (jnp.dot(yg, wga_ref[...], preferred_element_type=F32)
          * jax.nn.sigmoid(jnp.dot(yg, wgb_ref[...], preferred_element_type=F32)))
    merged = jax.nn.sigmoid(wide(g1_ref)) * ya + jax.nn.sigmoid(wide(g2_ref)) * yb
    x1 = x_ref[...] + jnp.dot(merged.astype(BF16), wo_ref[...], preferred_element_type=F32)
    x1_ref[...] = x1
    h2t_ref[...] = _rms(x1, gf_ref[...]).T.astype(BF16)


def _merge(x2d, ya, ys, z5f, w_lru_out, w_glu_a, w_glu_b, w_out, s5_d, g_ffn, tb):
    n = x2d.shape[0]
    tok = pl.BlockSpec((tb, D_MODEL), lambda i: (i, 0))
    slabs = pl.BlockSpec((N_TILES, tb, LANES), lambda i: (0, i, 0))
    zsl = lambda k: pl.BlockSpec((None, N_TILES, tb, LANES), lambda i: (k, 0, i, 0))
    wsp = pl.BlockSpec((D_MODEL, D_MODEL), lambda i: (0, 0))
    vec = pl.BlockSpec((1, D_MODEL), lambda i: (0, 0))
    return pl.pallas_call(
        _merge_kernel,
        out_shape=(jax.ShapeDtypeStruct((n, D_MODEL), F32), jax.ShapeDtypeStruct((D_MODEL, n), BF16)),
        grid=(n // tb,),
        in_specs=[tok, slabs, slabs, zsl(2), zsl(3), zsl(4), wsp, wsp, wsp, wsp, vec, vec],
        out_specs=(tok, pl.BlockSpec((D_MODEL, tb), lambda i: (0, i))),
        compiler_params=_params("arbitrary"),
        name="merge",
    )(x2d, ya, ys, z5f, z5f, z5f, w_lru_out, w_glu_a, w_glu_b, w_out,
      s5_d.reshape(1, D_MODEL), g_ffn.reshape(1, D_MODEL))


def _top16(s):
    keys = lax.broadcasted_iota(jnp.int32, s.shape, 0)
    slot = lax.broadcasted_iota(jnp.int32, (PEER_TOPK, s.shape[1]), 0)

    def body(k, carry):
        s, rank, vals = carry
        m = jnp.max(s, axis=0, keepdims=True)
        first = jnp.min(jnp.where(s == m, keys, N_KEYS), axis=0, keepdims=True)
        sel = keys == first
        s = jnp.where(sel, -jnp.inf, s)
        rank = jnp.where(sel, k, rank)
        vals = jnp.where(slot == k, m, vals)
        return s, rank, vals

    init = (s, jnp.full(s.shape, PEER_TOPK, jnp.int32), jnp.zeros((PEER_TOPK, s.shape[1]), F32))
    _, rank, vals = lax.fori_loop(0, PEER_TOPK, body, init)
    return vals, rank


def _pair_select(v1, v2):
    w = v1.shape[1]
    r8 = lax.broadcasted_iota(jnp.int32, (SUBLANES, w), 0)
    neg = -jnp.inf
    tiles = []

    big = PEER_TOPK * PEER_TOPK

    def fixed_a(a, b0, limit):
        sums = v1[a:a + 1] + v2[b0:b0 + SUBLANES]
        b = r8 + b0
        ok = b < limit
        tiles.append((jnp.where(ok, sums, neg), jnp.where(ok, a * PEER_TOPK + b, big), a, None))

    def fixed_b(b, a0, lo, hi):
        sums = v1[a0:a0 + SUBLANES] + v2[b:b + 1]
        a = r8 + a0
        ok = jnp.where(a >= lo, a, hi) < hi
        tiles.append((jnp.where(ok, sums, neg), jnp.where(ok, a * PEER_TOPK + b, big), None, a0))

    fixed_a(0, 0, 16); fixed_a(0, 8, 16); fixed_a(1, 0, 8); fixed_a(2, 0, 5); fixed_a(3, 0, 4)
    fixed_b(0, 0, 4, 16); fixed_b(0, 8, 4, 16); fixed_b(1, 0, 4, 8); fixed_b(2, 0, 4, 5)

    top = v1[0:1] + v2[0:1]
    sums = [t[0] for t in tiles]
    picked = [jnp.zeros((SUBLANES, w), F32) for _ in tiles]
    for _ in range(PEER_TOPK):
        m = functools.reduce(jnp.maximum, sums)
        m = jnp.max(m, axis=0, keepdims=True)
        cand = [jnp.where(s == m, t[1], big) for s, t in zip(sums, tiles)]
        first = jnp.min(functools.reduce(jnp.minimum, cand), axis=0, keepdims=True)
        for n, t in enumerate(tiles):
            sel = t[1] == first
            picked[n] = jnp.where(sel, 1.0, picked[n])
            sums[n] = jnp.where(sel, neg, sums[n])

    r16 = lax.broadcasted_iota(jnp.int32, (PEER_TOPK, w), 0)
    cnt = jnp.zeros((PEER_TOPK, w), F32)
    den = jnp.zeros((SUBLANES, w), F32)
    lo_rows = jnp.zeros((SUBLANES, w), F32)
    hi_rows = jnp.zeros((SUBLANES, w), F32)
    for n, t in enumerate(tiles):
        den = den + picked[n] * jnp.exp(jnp.where(picked[n] > 0, t[0], top) - top)
        if t[2] is not None:
            cnt = cnt + jnp.where(r16 == t[2], jnp.sum(picked[n], axis=0, keepdims=True), 0.0)
        elif t[3] == 0:
            lo_rows = lo_rows + picked[n]
        else:
            hi_rows = hi_rows + picked[n]
    cnt = cnt + jnp.concatenate([lo_rows, hi_rows], axis=0)
    return cnt, jnp.sum(den, axis=0, keepdims=True)


def _sort_pairs(n):
    pairs = []

    def merge(lo, m, r):
        step = 2 * r
        if step < m:
            merge(lo, m, step)
            merge(lo + r, m, step)
            pairs.extend((i, i + r) for i in range(lo + r, lo + m - r, step))
        else:
            pairs.append((lo, lo + r))

    def sort(lo, m):
        if m > 1:
            sort(lo, m // 2)
            sort(lo + m // 2, m // 2)
            merge(lo, m, 1)

    sort(0, n)
    return pairs


_SORT16 = _sort_pairs(PEER_TOPK)


def _exchange(v, i, j):
    a, b = v[i], v[j]
    if b is None:
        return
    if a is None:
        v[i], v[j] = b, None
        return
    v[i], v[j] = jnp.maximum(a, b), jnp.minimum(a, b)


def _sorted_top16(tiles):
    v = list(tiles) + [None] * (PEER_TOPK - len(tiles))
    for i, j in _SORT16:
        _exchange(v, i, j)
    for s in (1, 2, 4):
        r = [None if x is None else pltpu.roll(x, s, 0) for x in v]
        merged = []
        for k in range(PEER_TOPK):
            a, b = v[k], r[PEER_TOPK - 1 - k]
            merged.append(b if a is None else a if b is None else jnp.maximum(a, b))
        v = merged
        for d in (8, 4, 2, 1):
            for i in range(PEER_TOPK):
                if not i & d:
                    _exchange(v, i, i + d)
    return v


def _dup_words(x):
    bits = pltpu.bitcast(x.astype(BF16).astype(F32), jnp.uint32)
    return bits | (bits >> 16)


def _rows_bf16(words, rows):
    return pltpu.bitcast(jnp.broadcast_to(words, (rows // 2, words.shape[1])), BF16)


def _count_leading(test, rows):
    h8 = test(rows[7])
    h4 = test(jnp.where(h8, rows[11], rows[3]))
    h2 = test(jnp.where(h8, jnp.where(h4, rows[13], rows[9]), jnp.where(h4, rows[5], rows[1])))
    lo = jnp.where(h4, jnp.where(h2, rows[6], rows[4]), jnp.where(h2, rows[2], rows[0]))
    hi = jnp.where(h4, jnp.where(h2, rows[14], rows[12]), jnp.where(h2, rows[10], rows[8]))
    h1 = test(jnp.where(h8, hi, lo))
    n = ((jnp.where(h8, 8.0, 0.0) + jnp.where(h4, 4.0, 0.0))
         + (jnp.where(h2, 2.0, 0.0) + jnp.where(h1, 1.0, 0.0)))
    return jnp.where(test(rows[PEER_TOPK - 1]), float(PEER_TOPK), n)


def _route_fast(a1, a2):
    w = a1.shape[1]
    split = lambda a: [a[SUBLANES * k:SUBLANES * (k + 1)] for k in range(N_KEYS // SUBLANES)]
    t1 = _sorted_top16(split(a1))
    t2 = _sorted_top16(split(a2))
    row = lax.broadcasted_iota(jnp.int32, (SUBLANES, w), 0)

    def column(t, base):
        out = t[base]
        for r in range(1, SUBLANES):
            out = jnp.where(row == r, t[base + r], out)
        return out

    v1 = [column(t1, 0), column(t1, SUBLANES)]
    v2 = [column(t2, 0), column(t2, SUBLANES)]
    neg = -jnp.inf
    cands = [t1[0] + v2[0], t1[0] + v2[1], t1[1] + v2[0],
             jnp.where(row < 5, t1[2] + v2[0], neg), jnp.where(row < 4, t1[3] + v2[0], neg),
             jnp.where(row >= 4, v1[0] + t2[0], neg), v1[1] + t2[0],
             jnp.where(row >= 4, v1[0] + t2[1], neg), jnp.where(row == 4, v1[0] + t2[2], neg)]
    theta = _sorted_top16(cands)[PEER_TOPK - 1][0:1]
    top = t1[0] + t2[0]
    n_pairs = jnp.zeros((SUBLANES, w), F32)
    den = jnp.zeros((SUBLANES, w), F32)
    for c in cands:
        hit = c >= theta
        n_pairs = n_pairs + jnp.where(hit, 1.0, 0.0)
        den = den + jnp.where(hit, jnp.exp(jnp.where(hit, c, top) - top), 0.0)
    n_pairs = jnp.sum(n_pairs, axis=0, keepdims=True)
    inv_den = 1.0 / jnp.sum(den, axis=0, keepdims=True)

    tall = lambda t: pltpu.repeat(t, N_KEYS // SUBLANES, axis=0)
    in1 = a1 >= tall(t1[PEER_TOPK - 1])
    in2 = a2 >= tall(t2[PEER_TOPK - 1])
    n1 = jnp.sum(jnp.where(in1, 1.0, 0.0), axis=0, keepdims=True)
    n2 = jnp.sum(jnp.where(in2, 1.0, 0.0), axis=0, keepdims=True)
    theta_t = jnp.broadcast_to(theta, a1.shape)
    t2_t = [tall(t) for t in t2]
    ck = _count_leading(lambda v: a1 + v >= theta_t, t2_t)
    r2 = _count_leading(lambda v: v > a2, t2_t)
    ck = jnp.where(in1, ck, 0.0)
    pk = jnp.where(in1, jnp.exp(jnp.where(in1, a1, 0.0) - tall(t1[0])), 0.0)
    qk = jnp.where(in2, jnp.exp(jnp.where(in2, a2, 0.0) - t2_t[0]) * inv_den, 0.0)
    k = float(PEER_TOPK)
    flag = jnp.where((n1 != k) | (n2 != k) | (n_pairs != k), 1.0, 0.0)
    return pk, ck, qk, r2, flag


def _route_exact(a1, a2):
    v1, rank1 = _top16(a1)
    v2, rank2 = _top16(a2)
    cnt, den = _pair_select(v1, v2)
    in1 = rank1 < PEER_TOPK
    in2 = rank2 < PEER_TOPK
    ck = jnp.zeros(a1.shape, F32)
    for a in range(PEER_TOPK):
        ck = jnp.where(rank1 == a, cnt[a:a + 1], ck)
    pk = jnp.where(in1, jnp.exp(jnp.where(in1, a1, v1[0:1]) - v1[0:1]), 0.0)
    qk = jnp.where(in2, jnp.exp(jnp.where(in2, a2, v2[0:1]) - v2[0:1]) / den, 0.0)
    return pk, ck, qk, rank2.astype(F32)


def _route_kernel(h2t_ref, wqt_ref, keys_ref, p_ref, c_ref, q_ref, r2_ref, s_ref):
    qt = jnp.dot(wqt_ref[...], h2t_ref[...], preferred_element_type=F32)
    s_ref[0] = jnp.dot(keys_ref[0], qt[:PEER_HALF].astype(BF16), preferred_element_type=F32)
    s_ref[1] = jnp.dot(keys_ref[1], qt[PEER_HALF:].astype(BF16), preferred_element_type=F32)
    ntile = s_ref.shape[2] // LANES

    def store(cols, pk, ck, qk, r2):
        p_ref[:, cols] = _dup_words(pk)
        c_ref[:, cols] = _dup_words(ck)
        q_ref[:, cols] = qk.astype(q_ref.dtype)
        r2_ref[:, cols] = r2.astype(r2_ref.dtype)

    flag = jnp.zeros((1, LANES), F32)
    for lt in range(ntile):
        cols = slice(lt * LANES, (lt + 1) * LANES)
        pk, ck, qk, r2, f = _route_fast(s_ref[0, :, cols], s_ref[1, :, cols])
        store(cols, pk, ck, qk, r2)
        flag = jnp.maximum(flag, f)

    @pl.when(jnp.max(flag) > 0.0)
    def _():
        for lt in range(ntile):
            cols = slice(lt * LANES, (lt + 1) * LANES)
            store(cols, *_route_exact(s_ref[0, :, cols], s_ref[1, :, cols]))


def _route(h2t, wqt, keys, tb):
    n = h2t.shape[1]
    words = jax.ShapeDtypeStruct((PEER_HEADS, N_KEYS, n), jnp.uint32)
    halfs = jax.ShapeDtypeStruct((PEER_HEADS, N_KEYS, n), BF16)
    osp = pl.BlockSpec((None, N_KEYS, tb), lambda i, h: (h, 0, i))
    return pl.pallas_call(
        _route_kernel,
        out_shape=(words, words, halfs, halfs),
        grid=(n // tb, PEER_HEADS),
        in_specs=[pl.BlockSpec((D_MODEL, tb), lambda i, h: (0, i)),
                  pl.BlockSpec((2 * PEER_HALF, D_MODEL), lambda i, h: (h, 0)),
                  pl.BlockSpec((None, 2, N_KEYS, PEER_HALF), lambda i, h: (h, 0, 0, 0))],
        out_specs=(osp, osp, osp, osp),
        scratch_shapes=[pltpu.VMEM((2, N_KEYS, tb), F32)],
        compiler_params=_params("arbitrary", "arbitrary"),
        name="route",
    )(h2t, wqt, keys)


def _peer_kernel(h2t_ref, u_ref, vt_ref, p_ref, c_ref, q_ref, r2_ref, x1_ref, g_ref, y_ref, acc_ref,
                 act_a, act_b, *, rows_per_step):
    e = pl.program_id(1)
    last = pl.num_programs(1) - 1

    def score(dst_ref):
        dst_ref[...] = jnp.dot(u_ref[...], h2t_ref[...], preferred_element_type=F32)

    def combine(src_ref):
        parts = []
        for ii in range(rows_per_step):
            gate = None
            for h in range(PEER_HEADS):
                q = q_ref[h]
                hit = r2_ref[h] < _rows_bf16(c_ref[h, ii:ii + 1, :], N_KEYS)
                term = jnp.where(hit, q, jnp.zeros_like(q)) * _rows_bf16(p_ref[h, ii:ii + 1, :], N_KEYS)
                gate = term if gate is None else gate + term
            a = src_ref[ii * N_KEYS:(ii + 1) * N_KEYS, :].astype(BF16)
            parts.append(gate * _gelu(a))
        wt = jnp.concatenate(parts, axis=0)
        acc_ref[...] += jnp.dot(vt_ref[...], wt, preferred_element_type=F32)

    @pl.when(e == 0)
    def _():
        acc_ref[...] = jnp.zeros_like(acc_ref)
        score(act_a)

    @pl.when((e % 2 == 1) & (e < last))
    def _():
        score(act_b)
        combine(act_a)

    @pl.when((e % 2 == 0) & (e > 0) & (e < last))
    def _():
        score(act_a)
        combine(act_b)

    @pl.when(e == last)
    def _():
        combine(act_b if (N_KEYS // rows_per_step) % 2 == 0 else act_a)
        x2 = x1_ref[...] + acc_ref[...].T
        y_ref[...] = _rms(x2, g_ref[...])


def _peer(h2t, u, vt, pk, ck, qk, r2k, x1, g_final, tb, rows_per_step):
    n = h2t.shape[1]
    eb = rows_per_step * N_KEYS
    nblk = N_KEYS // rows_per_step
    scored = lambda e: jnp.minimum(e, nblk - 1)
    combined = lambda e: jnp.maximum(e - 1, 0)
    small = pl.BlockSpec((PEER_HEADS, rows_per_step, tb), lambda i, e: (0, combined(e), i))
    full = pl.BlockSpec((PEER_HEADS, N_KEYS, tb), lambda i, e: (0, 0, i))
    tok = pl.BlockSpec((tb, D_MODEL), lambda i, e: (i, 0))
    return pl.pallas_call(
        functools.partial(_peer_kernel, rows_per_step=rows_per_step),
        out_shape=jax.ShapeDtypeStruct((n, D_MODEL), F32),
        grid=(n // tb, nblk + 1),
        in_specs=[pl.BlockSpec((D_MODEL, tb), lambda i, e: (0, i)),
                  pl.BlockSpec((eb, D_MODEL), lambda i, e: (scored(e), 0)),
                  pl.BlockSpec((D_MODEL, eb), lambda i, e: (0, combined(e))),
                  small, small, full, full, tok,
                  pl.BlockSpec((1, D_MODEL), lambda i, e: (0, 0))],
        out_specs=tok,
        scratch_shapes=[pltpu.VMEM((D_MODEL, tb), F32), pltpu.VMEM((eb, tb), F32),
                        pltpu.VMEM((eb, tb), F32)],
        compiler_params=_params("arbitrary", "arbitrary"),
        name="peer",
    )(h2t, u, vt, pk, ck, qk, r2k, x1, g_final.reshape(1, D_MODEL))


def _layer(x, prm):
    nb, seq, _ = x.shape
    n = nb * seq
    x2d = x.reshape(n, D_MODEL)
    z5 = _in_proj(x2d, prm['g_mix'], prm['w_in'], 256)
    z5s = z5.reshape(z5.shape[0], N_TILES, nb, seq, LANES)
    ya = _lru(z5s, prm['conv_w'], prm['conv_b'], *prm['lru'], rt=512)
    ys = _s5(z5s, *prm['s5'], rb=min(256, seq // S5_CHUNK))
    x1, h2t = _merge(x2d, ya.reshape(N_TILES, n, LANES), ys.reshape(N_TILES, n, LANES), z5,
                     prm['w_lru_out'], prm['w_glu_a'], prm['w_glu_b'], prm['w_out'],
                     prm['s5_d'], prm['g_ffn'], 512)
    pk, ck, qk, r2k = _route(h2t, prm['wqt'], prm['keys'], min(1024, n))
    y = _peer(h2t, prm['u'], prm['vt'], pk, ck, qk, r2k, x1, prm['g_final'], 256, 32)
    return y.reshape(nb, seq, D_MODEL)


def kernel(x_prompt, x_sample, g_mix, w_in, conv_w, conv_b, lru_wr, lru_br, lru_wi, lru_bi, lru_lam,
           w_lru_out, s5_lam_re, s5_lam_im, s5_log_step, s5_b_re, s5_b_im, s5_c_re, s5_c_im, s5_d,
           w_glu_a, w_glu_b, w_out, g_ffn, w_query, sub_keys, expert_u, expert_v, g_final):
    depth = g_mix.shape[0]
    xp, xs = x_prompt, x_sample
    for l in range(depth):
        prm = {
            'g_mix': g_mix[l], 'w_in': w_in[l].astype(BF16),
            'conv_w': conv_w[l], 'conv_b': conv_b[l].reshape(1, D_MODEL),
            'lru': _lru_weights(lru_wr[l], lru_br[l], lru_wi[l], lru_bi[l], lru_lam[l]),
            's5': _s5_weights(s5_lam_re[l], s5_lam_im[l], s5_log_step[l], s5_b_re[l], s5_b_im[l],
                              s5_c_re[l], s5_c_im[l]),
            'w_lru_out': w_lru_out[l].astype(BF16), 'w_glu_a': w_glu_a[l].astype(BF16),
            'w_glu_b': w_glu_b[l].astype(BF16), 'w_out': w_out[l].astype(BF16),
            's5_d': s5_d[l], 'g_ffn': g_ffn[l],
            'wqt': w_query[l].astype(BF16).T,
            'keys': sub_keys[l].astype(BF16),
            'u': expert_u[l].astype(BF16), 'vt': expert_v[l].astype(BF16).T,
            'g_final': g_final,
        }
        assert depth == 1
        xp = _layer(xp, prm)
        xs = _layer(xs, prm)
    return (xp, xs)
```
